```python
import math
import jax, jax.numpy as jnp
from jax import lax
import numpy as np

D_MODEL = 2048
BATCH = 8
SEQ = 2048
DEPTH = 1

HEAD_DIM = 128
A_HEADS = 12
DILATED_PATTERNS = ((128, 1), (512, 4), (2048, 16))
B_HEADS = 8
M_HEADS = 4
M_HEAD_DIM = 256
MEM_LEN = 256
A_WIDTH = A_HEADS * HEAD_DIM
B_WIDTH = B_HEADS * HEAD_DIM
M_WIDTH = M_HEADS * M_HEAD_DIM
N_BRANCH = 3
BLK = 128
REL_BUCKETS = 32
REL_MAX_DIST = 2048
EPS = 1e-6
IN_WIDTHS = (A_WIDTH, A_WIDTH, A_WIDTH, A_WIDTH,
             B_WIDTH, B_WIDTH, B_WIDTH, B_WIDTH, B_HEADS,
             M_WIDTH, M_WIDTH, N_BRANCH * D_MODEL)
IN_COLS = 4 * A_WIDTH + 4 * B_WIDTH + B_HEADS + 2 * M_WIDTH + N_BRANCH * D_MODEL

kernel_name = "hybrid_gated_dilated_fox_memory_block"


def _rmsnorm(x, g):
    xf = x.astype(jnp.float32)
    y = xf * lax.rsqrt(jnp.mean(xf * xf, axis=-1, keepdims=True) + EPS)
    return (y * g.astype(jnp.float32)).astype(x.dtype)


def _split_cols(a, widths):
    outs, off = [], 0
    for w in widths:
        outs.append(a[..., off:off + w])
        off += w
    return outs


def _rel_bucket(dist):
    max_exact = REL_BUCKETS // 2
    d_f = jnp.maximum(dist, 1).astype(jnp.float32)
    large = max_exact + (jnp.log(d_f / max_exact) / math.log(REL_MAX_DIST / max_exact)
                         * (REL_BUCKETS - max_exact)).astype(jnp.int32)
    large = jnp.minimum(large, REL_BUCKETS - 1)
    return jnp.where(dist < max_exact, dist, large)


def _dilated_pattern(q, k, v, rel_table, window, dil):
    B, S, H, d = q.shape
    L = S // dil
    w_sub = window // dil
    nb = -(-L // BLK)
    Lp = nb * BLK
    N = B * dil

    def to_classes(a):
        return a.reshape(B, L, dil, H, d).transpose(0, 2, 1, 3, 4).reshape(N, L, H, d)

    qs = jnp.pad(to_classes(q), ((0, 0), (0, Lp - L), (0, 0), (0, 0)))
    ks = jnp.pad(to_classes(k), ((0, 0), (BLK, Lp - L), (0, 0), (0, 0)))
    vs = jnp.pad(to_classes(v), ((0, 0), (BLK, Lp - L), (0, 0), (0, 0)))
    qb = qs.reshape(N, nb, BLK, H, d)
    kb = ks.reshape(N, nb + 1, BLK, H, d)
    vb = vs.reshape(N, nb + 1, BLK, H, d)
    kw = jnp.concatenate([kb[:, :-1], kb[:, 1:]], axis=2)
    vw = jnp.concatenate([vb[:, :-1], vb[:, 1:]], axis=2)

    logits = jnp.einsum('nbqhd,nbkhd->nbhqk', qb, kw,
                        preferred_element_type=jnp.float32) * (1.0 / math.sqrt(d))
    delta = jnp.arange(BLK)[:, None] - jnp.arange(2 * BLK)[None, :] + BLK
    key_pos = (jnp.arange(nb) * BLK)[:, None, None] + jnp.arange(2 * BLK)[None, None, :] - BLK
    valid = (delta >= 0)[None] & (delta <= w_sub)[None] & (key_pos >= 0)
    bucket = _rel_bucket(jnp.clip(delta, 0, w_sub) * dil)
    bias = jnp.transpose(rel_table[bucket], (2, 0, 1)).astype(jnp.float32)
    logits = jnp.where(valid[None, :, None], logits + bias[None, None], -jnp.inf)
    mx = jnp.max(logits, axis=-1, keepdims=True)
    p = jnp.exp(logits - mx)
    den = jnp.sum(p, axis=-1)
    o = jnp.einsum('nbhqk,nbkhd->nbqhd', p.astype(v.dtype), vw,
                   preferred_element_type=jnp.float32) / jnp.swapaxes(den, 2, 3)[..., None]
    lse = jnp.swapaxes(mx[..., 0] + jnp.log(den), 2, 3)

    def from_classes(a):
        tail = a.shape[3:]
        a = a.reshape((N, Lp) + tail)[:, :L]
        a = a.reshape((B, dil, L) + tail)
        return jnp.swapaxes(a, 1, 2).reshape((B, S) + tail)

    return from_classes(o), from_classes(lse)


def _dilated_mixture(q, k, v, rel_table):
    outs, lses = [], []
    for window, dil in DILATED_PATTERNS:
        o, lse = _dilated_pattern(q, k, v, rel_table, window, dil)
        outs.append(o)
        lses.append(lse)
    alpha = jax.nn.softmax(jnp.stack(lses, axis=0), axis=0)
    return jnp.einsum('pbsh,pbshd->bshd', alpha, jnp.stack(outs, axis=0))


def _forgetting_attention(q, k, v, log_f):
    B, S, H, d = q.shape
    qt, kt, vt = jnp.swapaxes(q, 1, 2), jnp.swapaxes(k, 1, 2), jnp.swapaxes(v, 1, 2)
    c = jnp.cumsum(jnp.swapaxes(log_f, 1, 2), axis=-1)
    kpos = jnp.arange(S)
    scale = 1.0 / math.sqrt(d)

    def block(n):
        start = n * BLK
        qn = lax.dynamic_slice_in_dim(qt, start, BLK, axis=2)
        cq = lax.dynamic_slice_in_dim(c, start, BLK, axis=2)
        s = jnp.einsum('bhqd,bhkd->bhqk', qn, kt, preferred_element_type=jnp.float32) * scale
        s = s + cq[..., :, None] - c[..., None, :]
        qpos = start + jnp.arange(BLK)
        s = jnp.where(kpos[None, :] <= qpos[:, None], s, -jnp.inf)
        p = jax.nn.softmax(s, axis=-1)
        return jnp.einsum('bhqk,bhkd->bqhd', p.astype(vt.dtype), vt)

    o = lax.map(block, jnp.arange(S // BLK))
    return jnp.swapaxes(o, 0, 1).reshape(B, S, H, d)


def _memory_attention(q, k, v):
    s = jnp.einsum('bshd,bmhd->bhsm', q, k, preferred_element_type=jnp.float32) * (1.0 / math.sqrt(q.shape[-1]))
    p = jax.nn.softmax(s, axis=-1)
    return jnp.einsum('bhsm,bmhd->bshd', p.astype(v.dtype), v)


def setup_inputs(seed: int = 0) -> dict:
    key = jax.random.key(seed)
    ks = jax.random.split(key, 20)
    f32 = jnp.float32

    def gain(k, shape):
        return 1.0 + 0.02 * jax.random.normal(k, shape, f32)

    return {
        "x": jax.random.normal(ks[0], (BATCH, SEQ, D_MODEL), f32),
        "mem": jax.random.normal(ks[1], (BATCH, MEM_LEN, D_MODEL), f32),
        "norm_g": gain(ks[2], (DEPTH, D_MODEL)),
        "mem_norm_g": gain(ks[3], (DEPTH, D_MODEL)),
        "w_in": jax.random.normal(ks[4], (DEPTH, D_MODEL, IN_COLS), f32) * D_MODEL ** -0.5,
        "b_forget": jnp.linspace(2.0, 6.0, B_HEADS, dtype=f32)[None]
                    + 0.1 * jax.random.normal(ks[5], (DEPTH, B_HEADS), f32),
        "b_gate": 0.02 * jax.random.normal(ks[6], (DEPTH, N_BRANCH, D_MODEL), f32),
        "rel_bias": 0.1 * jax.random.normal(ks[7], (REL_BUCKETS, A_HEADS), f32),
        "q_norm_a": gain(ks[8], (DEPTH, HEAD_DIM)),
        "k_norm_a": gain(ks[9], (DEPTH, HEAD_DIM)),
        "q_norm_b": gain(ks[10], (DEPTH, HEAD_DIM)),
        "k_norm_b": gain(ks[11], (DEPTH, HEAD_DIM)),
        "q_norm_m": gain(ks[12], (DEPTH, M_HEAD_DIM)),
        "k_norm_m": gain(ks[13], (DEPTH, M_HEAD_DIM)),
        "w_mem_kv": jax.random.normal(ks[14], (DEPTH, D_MODEL, 2 * M_WIDTH), f32) * D_MODEL ** -0.5,
        "w_proj_a": jax.random.normal(ks[15], (DEPTH, A_WIDTH, D_MODEL), f32) * A_WIDTH ** -0.5,
        "w_proj_b": jax.random.normal(ks[16], (DEPTH, B_WIDTH, D_MODEL), f32) * B_WIDTH ** -0.5,
        "w_proj_m": jax.random.normal(ks[17], (DEPTH, M_WIDTH, D_MODEL), f32) * M_WIDTH ** -0.5,
        "w_out": jax.random.normal(ks[18], (DEPTH, D_MODEL, D_MODEL), f32) * D_MODEL ** -0.5,
    }


def reference(x, mem, norm_g, mem_norm_g, w_in, b_forget, b_gate, rel_bias,
              q_norm_a, k_norm_a, q_norm_b, k_norm_b, q_norm_m, k_norm_m,
              w_mem_kv, w_proj_a, w_proj_b, w_proj_m, w_out):
    B, S, _ = x.shape
    M = mem.shape[1]
    for l in range(DEPTH):
        h = _rmsnorm(x, norm_g[l])
        proj = h @ w_in[l]
        qa, ka, va, za, qb, kb, vb, zb, fb, qm, zm, gl = _split_cols(proj, IN_WIDTHS)

        qa = _rmsnorm(qa.reshape(B, S, A_HEADS, HEAD_DIM), q_norm_a[l])
        ka = _rmsnorm(ka.reshape(B, S, A_HEADS, HEAD_DIM), k_norm_a[l])
        va = va.reshape(B, S, A_HEADS, HEAD_DIM)
        oa = _dilated_mixture(qa, ka, va, rel_bias).reshape(B, S, A_WIDTH).astype(x.dtype)
        ua = (oa * jax.nn.silu(za)) @ w_proj_a[l]

        qb = _rmsnorm(qb.reshape(B, S, B_HEADS, HEAD_DIM), q_norm_b[l])
        kb = _rmsnorm(kb.reshape(B, S, B_HEADS, HEAD_DIM), k_norm_b[l])
        vb = vb.reshape(B, S, B_HEADS, HEAD_DIM)
        log_f = jax.nn.log_sigmoid(fb.astype(jnp.float32) + b_forget[l].astype(jnp.float32))
        ob = _forgetting_attention(qb, kb, vb, log_f).reshape(B, S, B_WIDTH).astype(x.dtype)
        ub = (ob * jax.nn.silu(zb)) @ w_proj_b[l]

        mh = _rmsnorm(mem, mem_norm_g[l])
        km, vm = _split_cols(mh @ w_mem_kv[l], (M_WIDTH, M_WIDTH))
        qm = _rmsnorm(qm.reshape(B, S, M_HEADS, M_HEAD_DIM), q_norm_m[l])
        km = _rmsnorm(km.reshape(B, M, M_HEADS, M_HEAD_DIM), k_norm_m[l])
        vm = vm.reshape(B, M, M_HEADS, M_HEAD_DIM)
        om = _memory_attention(qm, km, vm).reshape(B, S, M_WIDTH).astype(x.dtype)
        um = (om * jax.nn.silu(zm)) @ w_proj_m[l]

        g = jax.nn.sigmoid(gl.reshape(B, S, N_BRANCH, D_MODEL) + b_gate[l])
        merged = g[:, :, 0] * ua + g[:, :, 1] * ub + g[:, :, 2] * um
        x = x + (merged @ w_out[l]).astype(x.dtype)
    return x
```

```python
import functools
import math

import numpy as np
import jax
import jax.numpy as jnp
from jax import lax
from jax.experimental import pallas as pl
from jax.experimental.pallas import tpu as pltpu

F32 = jnp.float32
BF16 = jnp.bfloat16

D_MODEL = 2048
SEQ = 2048
HEAD_DIM = 128
A_HEADS = 12
B_HEADS = 8
M_HEADS = 4
M_HEAD_DIM = 256
MEM_LEN = 256
A_WIDTH = A_HEADS * HEAD_DIM
B_WIDTH = B_HEADS * HEAD_DIM
M_WIDTH = M_HEADS * M_HEAD_DIM
N_BRANCH = 3
BLK = 128
NBLK = SEQ // BLK
DILATED_PATTERNS = ((128, 1), (512, 4), (2048, 16))
REL_BUCKETS = 32
REL_MAX_DIST = 2048
EPS = 1e-6
NEG = -1e30

LANES = 128
VMEM_LIMIT = 56 * 1024 * 1024

COL_QA = 0
COL_KA = COL_QA + A_HEADS
COL_QB = COL_KA + A_HEADS
COL_KB = COL_QB + B_HEADS
COL_QM = COL_KB + B_HEADS
COL_VA = COL_QM + M_WIDTH // LANES
COL_ZA = COL_VA + A_HEADS
COL_VB = COL_ZA + A_HEADS
COL_ZB = COL_VB + B_HEADS
COL_ZM = COL_ZB + B_HEADS
COL_GL = COL_ZM + M_WIDTH // LANES
COL_END = COL_GL + N_BRANCH * D_MODEL // LANES
PROJ_COLS = COL_END * LANES

PROJ_TM = 1024
PROJ_TN = 512
PROJ_NB_HEAD128 = COL_QM * LANES // PROJ_TN
PROJ_NB_NORMED = COL_VA * LANES // PROJ_TN


def _log_sigmoid(x):
    return jnp.minimum(x, 0.0) - jnp.log1p(jnp.exp(-jnp.abs(x)))


def _rms_scale(a):
    return a * lax.rsqrt(jnp.mean(a * a, axis=-1, keepdims=True) + EPS)


def _proj_body(x_ref, g_ref, w_ref, gain_ref, wf_ref, bf_ref, out_ref, f_ref, h_ref):
    j = pl.program_id(1)

    @pl.when(j == 0)
    def _():
        hb = (_rms_scale(x_ref[...]) * g_ref[...]).astype(BF16)
        h_ref[...] = hb
        f = jnp.dot(hb, wf_ref[...], preferred_element_type=F32) + bf_ref[...]
        f_ref[...] = _log_sigmoid(f)

    acc = jnp.dot(h_ref[...], w_ref[...], preferred_element_type=F32)

    def head_normed(hd):
        gain = gain_ref[...]
        for s in range(PROJ_TN // hd):
            sl = slice(s * hd, (s + 1) * hd)
            out_ref[:, sl] = (_rms_scale(acc[:, sl]) * gain[:, sl]).astype(BF16)

    @pl.when(j < PROJ_NB_HEAD128)
    def _():
        head_normed(HEAD_DIM)

    @pl.when((j >= PROJ_NB_HEAD128) & (j < PROJ_NB_NORMED))
    def _():
        head_normed(M_HEAD_DIM)

    @pl.when(j >= PROJ_NB_NORMED)
    def _():
        out_ref[...] = acc.astype(BF16)


def _proj(x2, g, w, gain, wf, bfp):
    rows = x2.shape[0]
    n_gain_blocks = gain.shape[1] // PROJ_TN
    return pl.pallas_call(
        _proj_body,
        grid=(rows // PROJ_TM, PROJ_COLS // PROJ_TN),
        in_specs=[
            pl.BlockSpec((PROJ_TM, D_MODEL), lambda i, j: (i, 0)),
            pl.BlockSpec((1, D_MODEL), lambda i, j: (0, 0)),
            pl.BlockSpec((D_MODEL, PROJ_TN), lambda i, j: (0, j)),
            pl.BlockSpec((1, PROJ_TN), lambda i, j: (0, jnp.minimum(j, n_gain_blocks - 1))),
            pl.BlockSpec((D_MODEL, LANES), lambda i, j: (0, 0)),
            pl.BlockSpec((1, LANES), lambda i, j: (0, 0)),
        ],
        out_specs=[
            pl.BlockSpec((PROJ_TM, PROJ_TN), lambda i, j: (i, j)),
            pl.BlockSpec((PROJ_TM, LANES), lambda i, j: (i, 0)),
        ],
        out_shape=[
            jax.ShapeDtypeStruct((rows, PROJ_COLS), BF16),
            jax.ShapeDtypeStruct((rows, LANES), F32),
        ],
        scratch_shapes=[pltpu.VMEM((PROJ_TM, D_MODEL), BF16)],
        compiler_params=pltpu.CompilerParams(
            dimension_semantics=("parallel", "arbitrary"), vmem_limit_bytes=VMEM_LIMIT),
        name="proj",
    )(x2, g, w, gain, wf, bfp)


def _rel_bucket_np(dist):
    max_exact = REL_BUCKETS // 2
    d_f = np.maximum(dist, 1).astype(np.float32)
    large = max_exact + (np.log(d_f / np.float32(max_exact)) / np.float32(math.log(REL_MAX_DIST / max_exact))
                         * np.float32(REL_BUCKETS - max_exact)).astype(np.int32)
    large = np.minimum(large, REL_BUCKETS - 1)
    return np.where(dist < max_exact, dist, large).astype(np.int32)


def _bias_buckets():
    qi = np.arange(BLK)[:, None]
    kj = np.arange(BLK)[None, :]
    cur, prev = [], []
    for window, dil in DILATED_PATTERNS:
        w_sub = window // dil
        d_cur = qi - kj
        d_prev = qi - kj + BLK
        cur.append(np.where((d_cur >= 0) & (d_cur <= w_sub), _rel_bucket_np(np.clip(d_cur, 0, w_sub) * dil), -1))
        prev.append(np.where((d_prev >= 0) & (d_prev <= w_sub), _rel_bucket_np(np.clip(d_prev, 0, w_sub) * dil), -1))
    return np.stack(cur).astype(np.int32), np.stack(prev).astype(np.int32)


def _bias_body(tbl_ref, bc_ref, bp_ref, oc_ref, op_ref):
    h = pl.program_id(1)
    for b_ref, o_ref in ((bc_ref, oc_ref), (bp_ref, op_ref)):
        bucket = b_ref[...]
        out = jnp.full(bucket.shape, NEG, F32)
        for r in range(REL_BUCKETS):
            out = jnp.where(bucket == r, tbl_ref[r, h], out)
        o_ref[...] = out


def _bias_tiles(rel_bias):
    bc, bp = _bias_buckets()
    n_pat = len(DILATED_PATTERNS)
    tile = pl.BlockSpec((None, BLK, BLK), lambda p, h: (p, 0, 0))
    out_tile = pl.BlockSpec((None, None, BLK, BLK), lambda p, h: (p, h, 0, 0))
    return pl.pallas_call(
        _bias_body,
        grid=(n_pat, A_HEADS),
        in_specs=[pl.BlockSpec(memory_space=pltpu.SMEM), tile, tile],
        out_specs=[out_tile, out_tile],
        out_shape=[jax.ShapeDtypeStruct((n_pat, A_HEADS, BLK, BLK), F32)] * 2,
        name="rel_bias_tiles",
    )(rel_bias, jnp.asarray(bc), jnp.asarray(bp))


def _cumsum_body(f_ref, c_ref):
    ft = f_ref[...].T[:B_HEADS, :]
    pos = lax.broadcasted_iota(jnp.int32, ft.shape, 1)
    shift = 1
    while shift < SEQ:
        ft = ft + jnp.where(pos >= shift, pltpu.roll(ft, shift, axis=1), 0.0)
        shift *= 2
    c_ref[...] = ft


def _cumsum(logf, batch):
    return pl.pallas_call(
        _cumsum_body,
        grid=(batch,),
        in_specs=[pl.BlockSpec((SEQ, LANES), lambda b: (b, 0))],
        out_specs=pl.BlockSpec((None, B_HEADS, SEQ), lambda b: (b, 0, 0)),
        out_shape=jax.ShapeDtypeStruct((batch, B_HEADS, SEQ), F32),
        name="forget_cumsum",
    )(logf)


_NT = (((1,), (1,)), ((), ()))


def _bdot_nt(a, b):
    return jnp.einsum('nqd,nkd->nqk', a, b, preferred_element_type=F32)


def _bdot(a, b):
    return jnp.einsum('nqk,nkd->nqd', a, b, preferred_element_type=F32)


def _attn_a_body(q_ref, k_ref, v_ref, bc_ref, bp_ref, o_ref, qf, kf, vf, acc_s, m_s, l_s):
    qf[...] = q_ref[...].astype(F32)
    kf[...] = k_ref[...].astype(F32)
    vf[...] = v_ref[...].astype(F32)

    for p, (window, dil) in enumerate(DILATED_PATTERNS):
        cls_len = SEQ // dil
        cls_blocks = cls_len // BLK

        def class_major(ref_b, ref_f):
            if dil == 1:
                return ref_b[...].reshape(NBLK, BLK, HEAD_DIM)
            rows = [ref_f[pl.ds(c, cls_len, stride=dil), :] for c in range(dil)]
            return jnp.concatenate(rows, axis=0).astype(BF16).reshape(NBLK, BLK, HEAD_DIM)

        q3 = class_major(q_ref, qf)
        k3 = class_major(k_ref, kf)
        v3 = class_major(v_ref, vf)

        s_cur = _bdot_nt(q3, k3) + bc_ref[p][None]
        m = jnp.max(s_cur, axis=-1, keepdims=True)
        if cls_blocks > 1:
            s_prev = _bdot_nt(q3[1:], k3[:-1]) + bp_ref[p][None]
            if cls_blocks < NBLK:
                blk = lax.broadcasted_iota(jnp.int32, s_prev.shape, 0) + 1
                s_prev = jnp.where((blk & (cls_blocks - 1)) == 0, NEG, s_prev)
            m_prev = jnp.max(s_prev, axis=-1, keepdims=True)
            m = jnp.maximum(m, jnp.concatenate([jnp.full((1, BLK, 1), NEG, F32), m_prev], axis=0))
        p_cur = jnp.exp(s_cur - m)
        l = jnp.sum(p_cur, axis=-1, keepdims=True)
        acc = _bdot(p_cur.astype(BF16), v3)
        if cls_blocks > 1:
            p_prev = jnp.exp(s_prev - m[1:])
            l_prev = jnp.sum(p_prev, axis=-1, keepdims=True)
            l = l + jnp.concatenate([jnp.zeros((1, BLK, 1), F32), l_prev], axis=0)
            acc_prev = _bdot(p_prev.astype(BF16), v3[:-1])
            acc = acc + jnp.concatenate([jnp.zeros((1, BLK, HEAD_DIM), F32), acc_prev], axis=0)

        acc2 = acc.reshape(SEQ, HEAD_DIM)
        m2 = m.reshape(SEQ, 1)
        l2 = l.reshape(SEQ, 1)
        if dil == 1:
            acc_s[p] = acc2
            m_s[p] = m2
            l_s[p] = l2
        else:
            for c in range(dil):
                rows = slice(c * cls_len, (c + 1) * cls_len)
                acc_s[p, pl.ds(c, cls_len, stride=dil), :] = acc2[rows]
                m_s[p, pl.ds(c, cls_len, stride=dil), :] = m2[rows]
                l_s[p, pl.ds(c, cls_len, stride=dil), :] = l2[rows]

    n_pat = len(DILATED_PATTERNS)
    m_all = m_s[0]
    for p in range(1, n_pat):
        m_all = jnp.maximum(m_all, m_s[p])
    num = jnp.zeros((SEQ, HEAD_DIM), F32)
    den = jnp.zeros((SEQ, 1), F32)
    for p in range(n_pat):
        w = jnp.exp(m_s[p] - m_all)
        num = num + w * acc_s[p]
        den = den + w * l_s[p]
    o_ref[...] = (num * (1.0 / den)).astype(BF16)


def _attn_a(proj, bias_cur, bias_prev, batch):
    n_pat = len(DILATED_PATTERNS)
    head = lambda col0: pl.BlockSpec((SEQ, HEAD_DIM), lambda b, h: (b, col0 + h))
    bias = pl.BlockSpec((n_pat, None, BLK, BLK), lambda b, h: (0, h, 0, 0))
    return pl.pallas_call(
        _attn_a_body,
        grid=(batch, A_HEADS),
        in_specs=[head(COL_QA), head(COL_KA), head(COL_VA), bias, bias],
        out_specs=pl.BlockSpec((SEQ, HEAD_DIM), lambda b, h: (b, h)),
        out_shape=jax.ShapeDtypeStruct((batch * SEQ, A_WIDTH), BF16),
        scratch_shapes=[pltpu.VMEM((SEQ, HEAD_DIM), F32)] * 3 + [
            pltpu.VMEM((n_pat, SEQ, HEAD_DIM), F32),
            pltpu.VMEM((n_pat, SEQ, 1), F32),
            pltpu.VMEM((n_pat, SEQ, 1), F32),
        ],
        compiler_params=pltpu.CompilerParams(
            dimension_semantics=("parallel", "parallel"), vmem_limit_bytes=VMEM_LIMIT),
        name="attn_dilated",
    )(proj, proj, proj, bias_cur, bias_prev)


FOX_TQ = 256


def _attn_b_body(q_ref, k_ref, v_ref, c_ref, o_ref):
    h = pl.program_id(1)
    ck = c_ref[pl.ds(h, 1), :]
    qi = lax.broadcasted_iota(jnp.int32, (FOX_TQ, FOX_TQ), 0)
    kj = lax.broadcasted_iota(jnp.int32, (FOX_TQ, FOX_TQ), 1)
    causal = kj <= qi
    for i in range(SEQ // FOX_TQ):
        lo, hi = i * FOX_TQ, (i + 1) * FOX_TQ
        q = q_ref[lo:hi, :]
        s_diag = lax.dot_general(q, k_ref[lo:hi, :], _NT, preferred_element_type=F32) - ck[:, lo:hi]
        s_diag = jnp.where(causal, s_diag, NEG)
        m = jnp.max(s_diag, axis=-1, keepdims=True)
        if i > 0:
            s_off = lax.dot_general(q, k_ref[:lo, :], _NT, preferred_element_type=F32) - ck[:, :lo]
            m = jnp.maximum(m, jnp.max(s_off, axis=-1, keepdims=True))
        p_diag = jnp.exp(s_diag - m)
        l = jnp.sum(p_diag, axis=-1, keepdims=True)
        acc = jnp.dot(p_diag.astype(BF16), v_ref[lo:hi, :], preferred_element_type=F32)
        if i > 0:
            p_off = jnp.exp(s_off - m)
            l = l + jnp.sum(p_off, axis=-1, keepdims=True)
            acc = acc + jnp.dot(p_off.astype(BF16), v_ref[:lo, :], preferred_element_type=F32)
        o_ref[lo:hi, :] = (acc * (1.0 / l)).astype(BF16)


def _attn_b(proj, c, batch):
    head = lambda col0: pl.BlockSpec((SEQ, HEAD_DIM), lambda b, h: (b, col0 + h))
    return pl.pallas_call(
        _attn_b_body,
        grid=(batch, B_HEADS),
        in_specs=[head(COL_QB), head(COL_KB), head(COL_VB),
                  pl.BlockSpec((None, B_HEADS, SEQ), lambda b, h: (b, 0, 0))],
        out_specs=pl.BlockSpec((SEQ, HEAD_DIM), lambda b, h: (b, h)),
        out_shape=jax.ShapeDtypeStruct((batch * SEQ, B_WIDTH), BF16),
        compiler_params=pltpu.CompilerParams(
            dimension_semantics=("parallel", "parallel"), vmem_limit_bytes=VMEM_LIMIT),
        name="attn_forgetting",
    )(proj, proj, proj, c)


def _mem_kv_body(mem_ref, g_ref, w_ref, gain_ref, out_ref):
    hb = (_rms_scale(mem_ref[...]) * g_ref[...]).astype(BF16)
    acc = jnp.dot(hb, w_ref[...], preferred_element_type=F32)
    gain = gain_ref[...]
    for s in range(M_HEADS):
        sl = slice(s * M_HEAD_DIM, (s + 1) * M_HEAD_DIM)
        out_ref[:, sl] = (_rms_scale(acc[:, sl]) * gain[:, sl]).astype(BF16)
    out_ref[:, M_WIDTH:] = acc[:, M_WIDTH:].astype(BF16)


def _mem_kv(mem2, g, w, gain, batch):
    return pl.pallas_call(
        _mem_kv_body,
        grid=(batch,),
        in_specs=[
            pl.BlockSpec((MEM_LEN, D_MODEL), lambda b: (b, 0)),
            pl.BlockSpec((1, D_MODEL), lambda b: (0, 0)),
            pl.BlockSpec((D_MODEL, 2 * M_WIDTH), lambda b: (0, 0)),
            pl.BlockSpec((1, M_WIDTH), lambda b: (0, 0)),
        ],
        out_specs=pl.BlockSpec((MEM_LEN, 2 * M_WIDTH), lambda b: (b, 0)),
        out_shape=jax.ShapeDtypeStruct((batch * MEM_LEN, 2 * M_WIDTH), BF16),
        compiler_params=pltpu.CompilerParams(
            dimension_semantics=("parallel",), vmem_limit_bytes=VMEM_LIMIT),
        name="mem_kv",
    )(mem2, g, w, gain)


MEM_TQ = 512


def _attn_m_body(q_ref, k_ref, v_ref, o_ref):
    k = k_ref[...]
    v = v_ref[...]
    for i in range(SEQ // MEM_TQ):
        rows = slice(i * MEM_TQ, (i + 1) * MEM_TQ)
        s = lax.dot_general(q_ref[rows, :], k, _NT, preferred_element_type=F32)
        p = jnp.exp(s - jnp.max(s, axis=-1, keepdims=True))
        l = jnp.sum(p, axis=-1, keepdims=True)
        acc = jnp.dot(p.astype(BF16), v, preferred_element_type=F32)
        o_ref[rows, :] = (acc * (1.0 / l)).astype(BF16)


def _attn_m(proj, kv, batch):
    col_qm = COL_QM * LANES // M_HEAD_DIM
    return pl.pallas_call(
        _attn_m_body,
        grid=(batch, M_HEADS),
        in_specs=[
            pl.BlockSpec((SEQ, M_HEAD_DIM), lambda b, h: (b, col_qm + h)),
            pl.BlockSpec((MEM_LEN, M_HEAD_DIM), lambda b, h: (b, h)),
            pl.BlockSpec((MEM_LEN, M_HEAD_DIM), lambda b, h: (b, M_HEADS + h)),
        ],
        out_specs=pl.BlockSpec((SEQ, M_HEAD_DIM), lambda b, h: (b, h)),
        out_shape=jax.ShapeDtypeStruct((batch * SEQ, M_WIDTH), BF16),
        compiler_params=pltpu.CompilerParams(
            dimension_semantics=("parallel", "parallel"), vmem_limit_bytes=VMEM_LIMIT),
        name="attn_memory",
    )(proj, kv, kv)


MERGE_TM = 256


def _merge_body(x_ref, oa_ref, za_ref, ob_ref, zb_ref, om_ref, zm_ref, gl_ref, bg_ref,
                wa_ref, wb_ref, wm_ref, wo_ref, out_ref):
    def branch(o_ref, z_ref, w_ref):
        z = z_ref[...].astype(F32)
        gated = o_ref[...].astype(F32) * (z * jax.nn.sigmoid(z))
        return jnp.dot(gated.astype(BF16), w_ref[...], preferred_element_type=F32)

    merged = None
    for n, (o_ref, z_ref, w_ref) in enumerate(((oa_ref, za_ref, wa_ref), (ob_ref, zb_ref, wb_ref),
                                                (om_ref, zm_ref, wm_ref))):
        cols = slice(n * D_MODEL, (n + 1) * D_MODEL)
        gate = jax.nn.sigmoid(gl_ref[:, cols].astype(F32) + bg_ref[:, cols])
        term = gate * branch(o_ref, z_ref, w_ref)
        merged = term if merged is None else merged + term
    out_ref[...] = x_ref[...] + jnp.dot(merged.astype(BF16), wo_ref[...], preferred_element_type=F32)


def _merge(x2, oa, ob, om, proj, bg, wa, wb, wm, wo):
    rows = x2.shape[0]
    row_blk = lambda width, colblk: pl.BlockSpec((MERGE_TM, width), lambda i: (i, colblk))
    resident = lambda shape: pl.BlockSpec(shape, lambda i: (0, 0), pipeline_mode=pl.Buffered(1))
    assert (COL_ZA * LANES) % A_WIDTH == 0 and (COL_ZB * LANES) % B_WIDTH == 0
    assert (COL_ZM * LANES) % M_WIDTH == 0 and (COL_GL * LANES) % (N_BRANCH * D_MODEL) == 0
    return pl.pallas_call(
        _merge_body,
        grid=(rows // MERGE_TM,),
        in_specs=[
            row_blk(D_MODEL, 0),
            row_blk(A_WIDTH, 0), row_blk(A_WIDTH, COL_ZA * LANES // A_WIDTH),
            row_blk(B_WIDTH, 0), row_blk(B_WIDTH, COL_ZB * LANES // B_WIDTH),
            row_blk(M_WIDTH, 0), row_blk(M_WIDTH, COL_ZM * LANES // M_WIDTH),
            row_blk(N_BRANCH * D_MODEL, COL_GL * LANES // (N_BRANCH * D_MODEL)),
            resident((1, N_BRANCH * D_MODEL)),
            resident((A_WIDTH, D_MODEL)), resident((B_WIDTH, D_MODEL)), resident((M_WIDTH, D_MODEL)),
            resident((D_MODEL, D_MODEL)),
        ],
        out_specs=row_blk(D_MODEL, 0),
        out_shape=jax.ShapeDtypeStruct((rows, D_MODEL), F32),
        compiler_params=pltpu.CompilerParams(
            dimension_semantics=("parallel",), vmem_limit_bytes=VMEM_LIMIT),
        name="merge_out",
    )(x2, oa, proj, ob, proj, om, proj, proj, bg, wa, wb, wm, wo)


def _split_w_in(w):
    widths = (A_WIDTH, A_WIDTH, A_WIDTH, A_WIDTH, B_WIDTH, B_WIDTH, B_WIDTH, B_WIDTH, B_HEADS,
              M_WIDTH, M_WIDTH, N_BRANCH * D_MODEL)
    outs, off = [], 0
    for width in widths:
        outs.append(w[:, off:off + width])
        off += width
    return outs


def _layer(x2, mem2, batch, norm_g, mem_norm_g, w_in, b_forget, b_gate, rel_bias, q_norm_a, k_norm_a,
           q_norm_b, k_norm_b, q_norm_m, k_norm_m, w_mem_kv, w_proj_a, w_proj_b, w_proj_m, w_out):
    wqa, wka, wva, wza, wqb, wkb, wvb, wzb, wfb, wqm, wzm, wgl = _split_w_in(w_in)
    w_main = jnp.concatenate([wqa, wka, wqb, wkb, wqm, wva, wza, wvb, wzb, wzm, wgl], axis=1).astype(BF16)
    w_f = jnp.pad(wfb, ((0, 0), (0, LANES - B_HEADS))).astype(BF16)
    b_f = jnp.pad(b_forget.astype(F32), (0, LANES - B_HEADS))[None]
    scale = 1.0 / math.sqrt(HEAD_DIM)
    scale_m = 1.0 / math.sqrt(M_HEAD_DIM)
    gain = jnp.concatenate([
        jnp.tile(q_norm_a.astype(F32) * scale, A_HEADS), jnp.tile(k_norm_a.astype(F32), A_HEADS),
        jnp.tile(q_norm_b.astype(F32) * scale, B_HEADS), jnp.tile(k_norm_b.astype(F32), B_HEADS),
        jnp.tile(q_norm_m.astype(F32) * scale_m, M_HEADS)])[None]

    proj, logf = _proj(x2, norm_g.astype(F32)[None], w_main, gain, w_f, b_f)

    bias_cur, bias_prev = _bias_tiles(rel_bias.astype(F32))
    oa = _attn_a(proj, bias_cur, bias_prev, batch)

    c = _cumsum(logf, batch)
    ob = _attn_b(proj, c, batch)

    kv = _mem_kv(mem2, mem_norm_g.astype(F32)[None], w_mem_kv.astype(BF16),
                 jnp.tile(k_norm_m.astype(F32), M_HEADS)[None], batch)
    om = _attn_m(proj, kv, batch)

    return _merge(x2, oa, ob, om, proj, b_gate.astype(F32).reshape(1, N_BRANCH * D_MODEL),
                  w_proj_a.astype(BF16), w_proj_b.astype(BF16), w_proj_m.astype(BF16), w_out.astype(BF16))


def kernel(x, mem, norm_g, mem_norm_g, w_in, b_forget, b_gate, rel_bias, q_norm_a, k_norm_a, q_norm_b, k_norm_b,
           q_norm_m, k_norm_m, w_mem_kv, w_proj_a, w_proj_b, w_proj_m, w_out):
    batch, seq, d_model = x.shape
    assert (seq, d_model) == (SEQ, D_MODEL) and mem.shape == (batch, MEM_LEN, D_MODEL)
    x2 = x.reshape(batch * seq, d_model)
    mem2 = mem.reshape(batch * MEM_LEN, d_model)
    depth = w_in.shape[0]
    for l in range(depth):
        x2 = _layer(x2, mem2, batch, norm_g[l], mem_norm_g[l], w_in[l], b_forget[l], b_gate[l], rel_bias,
                    q_norm_a[l], k_norm_a[l], q_norm_b[l], k_norm_b[l], q_norm_m[l], k_norm_m[l],
                    w_mem_kv[l], w_proj_a[l], w_proj_b[l], w_proj_m[l], w_out[l])
    return x2.reshape(batch, seq, d_model)
```

```python
import functools
import math

import numpy as np
import jax
import jax.numpy as jnp
from jax import lax
from jax.experimental import pallas as pl
from jax.experimental.pallas import tpu as pltpu

F32 = jnp.float32
BF16 = jnp.bfloat16

D_MODEL = 2048
SEQ = 2048
HEAD_DIM = 128
A_HEADS = 12
B_HEADS = 8
M_HEADS = 4
M_HEAD_DIM = 256
MEM_LEN = 256
A_WIDTH = A_HEADS * HEAD_DIM
B_WIDTH = B_HEADS * HEAD_DIM
M_WIDTH = M_HEADS * M_HEAD_DIM
N_BRANCH = 3
BLK = 128
NBLK = SEQ // BLK
DILATED_PATTERNS = ((128, 1), (512, 4), (2048, 16))
REL_BUCKETS = 32
REL_MAX_DIST = 2048
EPS = 1e-6
NEG = -1e30
LOG2E = math.log2(math.e)

LANES = 128
VMEM_LIMIT = 56 * 1024 * 1024

COL_QA = 0
COL_KA = COL_QA + A_HEADS
COL_QB = COL_KA + A_HEADS
COL_KB = COL_QB + B_HEADS
COL_QM = COL_KB + B_HEADS
COL_VA = COL_QM + M_WIDTH // LANES
COL_ZA = COL_VA + A_HEADS
COL_VB = COL_ZA + A_HEADS
COL_ZB = COL_VB + B_HEADS
COL_ZM = COL_ZB + B_HEADS
COL_GL = COL_ZM + M_WIDTH // LANES
COL_END = COL_GL + N_BRANCH * D_MODEL // LANES
PROJ_COLS = COL_END * LANES

PROJ_TM = 1024
PROJ_TN = 512
PROJ_NB_HEAD128 = COL_QM * LANES // PROJ_TN
PROJ_NB_NORMED = COL_VA * LANES // PROJ_TN


def _log_sigmoid(x):
    return jnp.minimum(x, 0.0) - jnp.log1p(jnp.exp(-jnp.abs(x)))


def _rms_scale(a):
    return a * lax.rsqrt(jnp.mean(a * a, axis=-1, keepdims=True) + EPS)


def _proj_body(x_ref, g_ref, w_ref, gain_ref, wf_ref, bf_ref, out_ref, f_ref, h_ref):
    j = pl.program_id(1)

    @pl.when(j == 0)
    def _():
        hb = (_rms_scale(x_ref[...]) * g_ref[...]).astype(BF16)
        h_ref[...] = hb
        f = jnp.dot(hb, wf_ref[...], preferred_element_type=F32) + bf_ref[...]
        f_ref[...] = _log_sigmoid(f)

    acc = jnp.dot(h_ref[...], w_ref[...], preferred_element_type=F32)

    def head_normed(hd):
        gain = gain_ref[...]
        for s in range(PROJ_TN // hd):
            sl = slice(s * hd, (s + 1) * hd)
            out_ref[:, sl] = (_rms_scale(acc[:, sl]) * gain[:, sl]).astype(BF16)

    @pl.when(j < PROJ_NB_HEAD128)
    def _():
        head_normed(HEAD_DIM)

    @pl.when((j >= PROJ_NB_HEAD128) & (j < PROJ_NB_NORMED))
    def _():
        head_normed(M_HEAD_DIM)

    @pl.when(j >= PROJ_NB_NORMED)
    def _():
        out_ref[...] = acc.astype(BF16)


def _proj(x2, g, w, gain, wf, bfp):
    rows = x2.shape[0]
    n_gain_blocks = gain.shape[1] // PROJ_TN
    return pl.pallas_call(
        _proj_body,
        grid=(rows // PROJ_TM, PROJ_COLS // PROJ_TN),
        in_specs=[
            pl.BlockSpec((PROJ_TM, D_MODEL), lambda i, j: (i, 0)),
            pl.BlockSpec((1, D_MODEL), lambda i, j: (0, 0)),
            pl.BlockSpec((D_MODEL, PROJ_TN), lambda i, j: (0, j)),
            pl.BlockSpec((1, PROJ_TN), lambda i, j: (0, jnp.minimum(j, n_gain_blocks - 1))),
            pl.BlockSpec((D_MODEL, LANES), lambda i, j: (0, 0)),
            pl.BlockSpec((1, LANES), lambda i, j: (0, 0)),
        ],
        out_specs=[
            pl.BlockSpec((PROJ_TM, PROJ_TN), lambda i, j: (i, j)),
            pl.BlockSpec((PROJ_TM, LANES), lambda i, j: (i, 0)),
        ],
        out_shape=[
            jax.ShapeDtypeStruct((rows, PROJ_COLS), BF16),
            jax.ShapeDtypeStruct((rows, LANES), F32),
        ],
        scratch_shapes=[pltpu.VMEM((PROJ_TM, D_MODEL), BF16)],
        compiler_params=pltpu.CompilerParams(
            dimension_semantics=("parallel", "arbitrary"), vmem_limit_bytes=VMEM_LIMIT),
        name="proj",
    )(x2, g, w, gain, wf, bfp)


def _rel_bucket_np(dist):
    max_exact = REL_BUCKETS // 2
    d_f = np.maximum(dist, 1).astype(np.float32)
    large = max_exact + (np.log(d_f / np.float32(max_exact)) / np.float32(math.log(REL_MAX_DIST / max_exact))
                         * np.float32(REL_BUCKETS - max_exact)).astype(np.int32)
    large = np.minimum(large, REL_BUCKETS - 1)
    return np.where(dist < max_exact, dist, large).astype(np.int32)


def _bias_buckets():
    qi = np.arange(BLK)[:, None]
    kj = np.arange(2 * BLK)[None, :]
    delta = qi - kj + BLK
    tiles = []
    for window, dil in DILATED_PATTERNS:
        w_sub = window // dil
        valid = (delta >= 0) & (delta <= w_sub)
        tiles.append(np.where(valid, _rel_bucket_np(np.clip(delta, 0, w_sub) * dil), -1))
    return np.stack(tiles).astype(np.int32)


def _bias_body(tbl_ref, bucket_ref, out_ref):
    h = pl.program_id(1)
    bucket = bucket_ref[...]
    out = jnp.full(bucket.shape, NEG, F32)
    for r in range(REL_BUCKETS):
        out = jnp.where(bucket == r, tbl_ref[r, h] * LOG2E, out)
    out_ref[...] = out


def _bias_tiles(rel_bias):
    n_pat = len(DILATED_PATTERNS)
    return pl.pallas_call(
        _bias_body,
        grid=(n_pat, A_HEADS),
        in_specs=[pl.BlockSpec(memory_space=pltpu.SMEM),
                  pl.BlockSpec((None, BLK, 2 * BLK), lambda p, h: (p, 0, 0))],
        out_specs=pl.BlockSpec((None, None, BLK, 2 * BLK), lambda p, h: (p, h, 0, 0)),
        out_shape=jax.ShapeDtypeStruct((n_pat, A_HEADS, BLK, 2 * BLK), F32),
        name="rel_bias_tiles",
    )(rel_bias, jnp.asarray(_bias_buckets()))


def _cumsum_body(f_ref, c_ref):
    ft = f_ref[...].T[:B_HEADS, :]
    pos = lax.broadcasted_iota(jnp.int32, ft.shape, 1)
    shift = 1
    while shift < SEQ:
        ft = ft + jnp.where(pos >= shift, pltpu.roll(ft, shift, axis=1), 0.0)
        shift *= 2
    c_ref[...] = ft * LOG2E


def _cumsum(logf, batch):
    return pl.pallas_call(
        _cumsum_body,
        grid=(batch,),
        in_specs=[pl.BlockSpec((SEQ, LANES), lambda b: (b, 0))],
        out_specs=pl.BlockSpec((None, B_HEADS, SEQ), lambda b: (b, 0, 0)),
        out_shape=jax.ShapeDtypeStruct((batch, B_HEADS, SEQ), F32),
        name="forget_cumsum",
    )(logf)


_NT = (((1,), (1,)), ((), ()))


P3_PITCH = BLK + 8


def _scores(q, k, bias):
    return lax.dot_general(q, k, _NT, preferred_element_type=F32) + bias


def _softmax_pv(s, v):
    m = jnp.max(s, axis=-1, keepdims=True)
    pe = jnp.exp2(s - m)
    l = jnp.sum(pe, axis=-1, keepdims=True)
    return jnp.dot(pe.astype(BF16), v, preferred_element_type=F32), m, l


def _emit_pipelined(tasks, lookahead):
    pending = []
    for start, finish in tasks:
        pending.append((finish, start()))
        if len(pending) > lookahead:
            fin, state = pending.pop(0)
            fin(*state)
    for fin, state in pending:
        fin(*state)


ATTN_A_LOOKAHEAD = 3


def _attn_a_body(q_ref, k_ref, v_ref, bias_ref, o_ref, nat, cm4, res2, res3):
    dil4, dil16 = DILATED_PATTERNS[1][1], DILATED_PATTERNS[2][1]
    cls_len = SEQ // dil4
    for i, ref in enumerate((q_ref, k_ref, v_ref)):
        nat[i] = ref[...].astype(F32)
        for c in range(dil4):
            cm4[i, c * cls_len:(c + 1) * cls_len, :] = nat[i, pl.ds(c, cls_len, stride=dil4), :]

    def operands(q_rows, kv_rows):
        return (cm4[0, q_rows, :].astype(BF16), cm4[1, kv_rows, :].astype(BF16), cm4[2, kv_rows, :].astype(BF16))

    def save(res, rows, acc, m, l):
        res[0, rows, :] = acc
        res[1, rows, :] = jnp.broadcast_to(m, acc.shape)
        res[2, rows, :] = jnp.broadcast_to(l, acc.shape)

    tasks = []

    def add_task(q_rows, kv_rows, bias, res, out_rows):
        def start():
            q, k, v = operands(q_rows, kv_rows)
            return _scores(q, k, bias()), v

        tasks.append((start, lambda s, v: save(res, out_rows, *_softmax_pv(s, v))))

    for c in range(dil4):
        for n in range(cls_len // BLK):
            lo = c * cls_len + n * BLK
            out_rows = pl.ds(n * BLK * dil4 + c, BLK, stride=dil4)
            if n == 0:
                add_task(slice(lo, lo + BLK), slice(lo, lo + BLK), lambda: bias_ref[1, :, BLK:], res2, out_rows)
            else:
                add_task(slice(lo, lo + BLK), slice(lo - BLK, lo + BLK), lambda: bias_ref[1], res2, out_rows)

    for c in range(dil16):
        rows = pl.ds((c % dil4) * cls_len + c // dil4, BLK, stride=dil16 // dil4)
        add_task(rows, rows, lambda: bias_ref[2, :, BLK:], res3, slice(c * P3_PITCH, c * P3_PITCH + BLK))

    def token_order(i, g):
        per_class = BLK // dil16
        return jnp.concatenate(
            [res3[i, pl.ds(g * per_class + j, dil16, stride=P3_PITCH), :] for j in range(per_class)], axis=0)

    def add_merge_task(g):
        rows = slice(g * BLK, (g + 1) * BLK)
        keys = rows if g == 0 else slice((g - 1) * BLK, (g + 1) * BLK)

        def start():
            bias = bias_ref[0, :, BLK:] if g == 0 else bias_ref[0]
            return _scores(q_ref[rows, :], k_ref[keys, :], bias), v_ref[keys, :]

        def finish(s, v):
            acc1, m1, l1 = _softmax_pv(s, v)
            m2, m3 = res2[1, rows, :], token_order(1, g)
            top = jnp.maximum(jnp.maximum(m1, m2), m3)
            w1, w2, w3 = jnp.exp2(m1 - top), jnp.exp2(m2 - top), jnp.exp2(m3 - top)
            num = w1 * acc1 + w2 * res2[0, rows, :] + w3 * token_order(0, g)
            den = w1 * l1 + w2 * res2[2, rows, :] + w3 * token_order(2, g)
            o_ref[rows, :] = (num * (1.0 / den)).astype(BF16)

        tasks.append((start, finish))

    for g in range(NBLK):
        add_merge_task(g)
    _emit_pipelined(tasks, ATTN_A_LOOKAHEAD)


def _attn_a(proj, bias, batch):
    n_pat = len(DILATED_PATTERNS)
    head = lambda col0: pl.BlockSpec((SEQ, HEAD_DIM), lambda b, h: (b, col0 + h))
    p3_rows = DILATED_PATTERNS[2][1] * P3_PITCH
    return pl.pallas_call(
        _attn_a_body,
        grid=(batch, A_HEADS),
        in_specs=[head(COL_QA), head(COL_KA), head(COL_VA),
                  pl.BlockSpec((n_pat, None, BLK, 2 * BLK), lambda b, h: (0, h, 0, 0))],
        out_specs=pl.BlockSpec((SEQ, HEAD_DIM), lambda b, h: (b, h)),
        out_shape=jax.ShapeDtypeStruct((batch * SEQ, A_WIDTH), BF16),
        scratch_shapes=[pltpu.VMEM((3, SEQ, HEAD_DIM), F32)] * 3 + [pltpu.VMEM((3, p3_rows, HEAD_DIM), F32)],
        compiler_params=pltpu.CompilerParams(
            dimension_semantics=("parallel", "parallel"), vmem_limit_bytes=VMEM_LIMIT),
        name="attn_dilated",
    )(proj, proj, proj, bias)


FOX_TQ = 256


def _attn_b_body(q_ref, k_ref, v_ref, c_ref, o_ref):
    h = pl.program_id(1)
    ck = c_ref[pl.ds(h, 1), :]
    qi = lax.broadcasted_iota(jnp.int32, (FOX_TQ, FOX_TQ), 0)
    kj = lax.broadcasted_iota(jnp.int32, (FOX_TQ, FOX_TQ), 1)
    causal = kj <= qi

    def task(i):
        lo, hi = i * FOX_TQ, (i + 1) * FOX_TQ

        def start():
            q = q_ref[lo:hi, :]
            s_diag = lax.dot_general(q, k_ref[lo:hi, :], _NT, preferred_element_type=F32)
            s_off = lax.dot_general(q, k_ref[:lo, :], _NT, preferred_element_type=F32) if i > 0 else None
            return s_diag, s_off

        def finish(s_diag, s_off):
            s_diag = jnp.where(causal, s_diag - ck[:, lo:hi], NEG)
            m = jnp.max(s_diag, axis=-1, keepdims=True)
            if i > 0:
                s_off = s_off - ck[:, :lo]
                m = jnp.maximum(m, jnp.max(s_off, axis=-1, keepdims=True))
            p_diag = jnp.exp2(s_diag - m)
            l = jnp.sum(p_diag, axis=-1, keepdims=True)
            acc = jnp.dot(p_diag.astype(BF16), v_ref[lo:hi, :], preferred_element_type=F32)
            if i > 0:
                p_off = jnp.exp2(s_off - m)
                l = l + jnp.sum(p_off, axis=-1, keepdims=True)
                acc = acc + jnp.dot(p_off.astype(BF16), v_ref[:lo, :], preferred_element_type=F32)
            o_ref[lo:hi, :] = (acc * (1.0 / l)).astype(BF16)

        return start, finish

    _emit_pipelined([task(i) for i in range(SEQ // FOX_TQ)], 1)


def _attn_b(proj, c, batch):
    head = lambda col0: pl.BlockSpec((SEQ, HEAD_DIM), lambda b, h: (b, col0 + h))
    return pl.pallas_call(
        _attn_b_body,
        grid=(batch, B_HEADS),
        in_specs=[head(COL_QB), head(COL_KB), head(COL_VB),
                  pl.BlockSpec((None, B_HEADS, SEQ), lambda b, h: (b, 0, 0))],
        out_specs=pl.BlockSpec((SEQ, HEAD_DIM), lambda b, h: (b, h)),
        out_shape=jax.ShapeDtypeStruct((batch * SEQ, B_WIDTH), BF16),
        compiler_params=pltpu.CompilerParams(
            dimension_semantics=("parallel", "parallel"), vmem_limit_bytes=VMEM_LIMIT),
        name="attn_forgetting",
    )(proj, proj, proj, c)


def _mem_kv_body(mem_ref, g_ref, w_ref, gain_ref, out_ref):
    hb = (_rms_scale(mem_ref[...]) * g_ref[...]).astype(BF16)
    acc = jnp.dot(hb, w_ref[...], preferred_element_type=F32)
    gain = gain_ref[...]
    for s in range(M_HEADS):
        sl = slice(s * M_HEAD_DIM, (s + 1) * M_HEAD_DIM)
        out_ref[:, sl] = (_rms_scale(acc[:, sl]) * gain[:, sl]).astype(BF16)
    out_ref[:, M_WIDTH:] = acc[:, M_WIDTH:].astype(BF16)


def _mem_kv(mem2, g, w, gain, batch):
    return pl.pallas_call(
        _mem_kv_body,
        grid=(batch,),
        in_specs=[
            pl.BlockSpec((MEM_LEN, D_MODEL), lambda b: (b, 0)),
            pl.BlockSpec((1, D_MODEL), lambda b: (0, 0)),
            pl.BlockSpec((D_MODEL, 2 * M_WIDTH), lambda b: (0, 0)),
            pl.BlockSpec((1, M_WIDTH), lambda b: (0, 0)),
        ],
        out_specs=pl.BlockSpec((MEM_LEN, 2 * M_WIDTH), lambda b: (b, 0)),
        out_shape=jax.ShapeDtypeStruct((batch * MEM_LEN, 2 * M_WIDTH), BF16),
        compiler_params=pltpu.CompilerParams(
            dimension_semantics=("parallel",), vmem_limit_bytes=VMEM_LIMIT),
        name="mem_kv",
    )(mem2, g, w, gain)


MEM_TQ = 512


def _attn_m_body(q_ref, k_ref, v_ref, o_ref):
    k = k_ref[...]
    v = v_ref[...]
    for i in range(SEQ // MEM_TQ):
        rows = slice(i * MEM_TQ, (i + 1) * MEM_TQ)
        s = lax.dot_general(q_ref[rows, :], k, _NT, preferred_element_type=F32)
        p = jnp.exp2(s - jnp.max(s, axis=-1, keepdims=True))
        l = jnp.sum(p, axis=-1, keepdims=True)
        acc = jnp.dot(p.astype(BF16), v, preferred_element_type=F32)
        o_ref[rows, :] = (acc * (1.0 / l)).astype(BF16)


def _attn_m(proj, kv, batch):
    col_qm = COL_QM * LANES // M_HEAD_DIM
    return pl.pallas_call(
        _attn_m_body,
        grid=(batch, M_HEADS),
        in_specs=[
            pl.BlockSpec((SEQ, M_HEAD_DIM), lambda b, h: (b, col_qm + h)),
            pl.BlockSpec((MEM_LEN, M_HEAD_DIM), lambda b, h: (b, h)),
            pl.BlockSpec((MEM_LEN, M_HEAD_DIM), lambda b, h: (b, M_HEADS + h)),
        ],
        out_specs=pl.BlockSpec((SEQ, M_HEAD_DIM), lambda b, h: (b, h)),
        out_shape=jax.ShapeDtypeStruct((batch * SEQ, M_WIDTH), BF16),
        compiler_params=pltpu.CompilerParams(
            dimension_semantics=("parallel", "parallel"), vmem_limit_bytes=VMEM_LIMIT),
        name="attn_memory",
    )(proj, kv, kv)


MERGE_TM = 256


def _merge_body(x_ref, oa_ref, za_ref, ob_ref, zb_ref, om_ref, zm_ref, gl_ref, bg_ref,
                wa_ref, wb_ref, wm_ref, wo_ref, out_ref):
    def branch(o_ref, z_ref, w_ref):
        z = z_ref[...].astype(F32)
        gated = o_ref[...].astype(F32) * (z * jax.nn.sigmoid(z))
        return jnp.dot(gated.astype(BF16), w_ref[...], preferred_element_type=F32)

    merged = None
    for n, (o_ref, z_ref, w_ref) in enumerate(((oa_ref, za_ref, wa_ref), (ob_ref, zb_ref, wb_ref),
                                                (om_ref, zm_ref, wm_ref))):
        cols = slice(n * D_MODEL, (n + 1) * D_MODEL)
        gate = jax.nn.sigmoid(gl_ref[:, cols].astype(F32) + bg_ref[:, cols])
        term = gate * branch(o_ref, z_ref, w_ref)
        merged = term if merged is None else merged + term
    out_ref[...] = x_ref[...] + jnp.dot(merged.astype(BF16), wo_ref[...], preferred_element_type=F32)


def _merge(x2, oa, ob, om, proj, bg, wa, wb, wm, wo):
    rows = x2.shape[0]
    row_blk = lambda width, colblk: pl.BlockSpec((MERGE_TM, width), lambda i: (i, colblk))
    resident = lambda shape: pl.BlockSpec(shape, lambda i: (0, 0), pipeline_mode=pl.Buffered(1))
    assert (COL_ZA * LANES) % A_WIDTH == 0 and (COL_ZB * LANES) % B_WIDTH == 0
    assert (COL_ZM * LANES) % M_WIDTH == 0 and (COL_GL * LANES) % (N_BRANCH * D_MODEL) == 0
    return pl.pallas_call(
        _merge_body,
        grid=(rows // MERGE_TM,),
        in_specs=[
            row_blk(D_MODEL, 0),
            row_blk(A_WIDTH, 0), row_blk(A_WIDTH, COL_ZA * LANES // A_WIDTH),
            row_blk(B_WIDTH, 0), row_blk(B_WIDTH, COL_ZB * LANES // B_WIDTH),
            row_blk(M_WIDTH, 0), row_blk(M_WIDTH, COL_ZM * LANES // M_WIDTH),
            row_blk(N_BRANCH * D_MODEL, COL_GL * LANES // (N_BRANCH * D_MODEL)),
            resident((1, N_BRANCH * D_MODEL)),
            resident((A_WIDTH, D_MODEL)), resident((B_WIDTH, D_MODEL)), resident((M_WIDTH, D_MODEL)),
            resident((D_MODEL, D_MODEL)),
        ],
        out_specs=row_blk(D_MODEL, 0),
        out_shape=jax.ShapeDtypeStruct((rows, D_MODEL), F32),
        compiler_params=pltpu.CompilerParams(
            dimension_semantics=("parallel",), vmem_limit_bytes=VMEM_LIMIT),
        name="merge_out",
    )(x2, oa, proj, ob, proj, om, proj, proj, bg, wa, wb, wm, wo)


def _split_w_in(w):
    widths = (A_WIDTH, A_WIDTH, A_WIDTH, A_WIDTH, B_WIDTH, B_WIDTH, B_WIDTH, B_WIDTH, B_HEADS,
              M_WIDTH, M_WIDTH, N_BRANCH * D_MODEL)
    outs, off = [], 0
    for width in widths:
        outs.append(w[:, off:off + width])
        off += width
    return outs


def _layer(x2, mem2, batch, norm_g, mem_norm_g, w_in, b_forget, b_gate, rel_bias, q_norm_a, k_norm_a,
           q_norm_b, k_norm_b, q_norm_m, k_norm_m, w_mem_kv, w_proj_a, w_proj_b, w_proj_m, w_out):
    wqa, wka, wva, wza, wqb, wkb, wvb, wzb, wfb, wqm, wzm, wgl = _split_w_in(w_in)
    w_main = jnp.concatenate([wqa, wka, wqb, wkb, wqm, wva, wza, wvb, wzb, wzm, wgl], axis=1).astype(BF16)
    w_f = jnp.pad(wfb, ((0, 0), (0, LANES - B_HEADS))).astype(BF16)
    b_f = jnp.pad(b_forget.astype(F32), (0, LANES - B_HEADS))[None]
    scale = LOG2E / math.sqrt(HEAD_DIM)
    scale_m = LOG2E / math.sqrt(M_HEAD_DIM)
    gain = jnp.concatenate([
        jnp.tile(q_norm_a.astype(F32) * scale, A_HEADS), jnp.tile(k_norm_a.astype(F32), A_HEADS),
        jnp.tile(q_norm_b.astype(F32) * scale, B_HEADS), jnp.tile(k_norm_b.astype(F32), B_HEADS),
        jnp.tile(q_norm_m.astype(F32) * scale_m, M_HEADS)])[None]

    proj, logf = _proj(x2, norm_g.astype(F32)[None], w_main, gain, w_f, b_f)

    oa = _attn_a(proj, _bias_tiles(rel_bias.astype(F32)), batch)

    c = _cumsum(logf, batch)
    ob = _attn_b(proj, c, batch)

    kv = _mem_kv(mem2, mem_norm_g.astype(F32)[None], w_mem_kv.astype(BF16),
                 jnp.tile(k_norm_m.astype(F32), M_HEADS)[None], batch)
    om = _attn_m(proj, kv, batch)

    return _merge(x2, oa, ob, om, proj, b_gate.astype(F32).reshape(1, N_BRANCH * D_MODEL),
                  w_proj_a.astype(BF16), w_proj_b.astype(BF16), w_proj_m.astype(BF16), w_out.astype(BF16))


def kernel(x, mem, norm_g, mem_norm_g, w_in, b_forget, b_gate, rel_bias, q_norm_a, k_norm_a, q_norm_b, k_norm_b,
           q_norm_m, k_norm_m, w_mem_kv, w_proj_a, w_proj_b, w_proj_m, w_out):
    batch, seq, d_model = x.shape
    assert (seq, d_model) == (SEQ, D_MODEL) and mem.shape == (batch, MEM_LEN, D_MODEL)
    x2 = x.reshape(batch * seq, d_model)
    mem2 = mem.reshape(batch * MEM_LEN, d_model)
    depth = w_in.shape[0]
    for l in range(depth):
        x2 = _layer(x2, mem2, batch, norm_g[l], mem_norm_g[l], w_in[l], b_forget[l], b_gate[l], rel_bias,
                    q_norm_a[l], k_norm_a[l], q_norm_b[l], k_norm_b[l], q_norm_m[l], k_norm_m[l],
                    w_mem_kv[l], w_proj_a[l], w_proj_b[l], w_proj_m[l], w_out[l])
    return x2.reshape(batch, seq, d_model)
```

```python
import functools
import math

import numpy as np
import jax
import jax.numpy as jnp
from jax import lax
from jax.experimental import pallas as pl
from jax.experimental.pallas import tpu as pltpu

F32 = jnp.float32
BF16 = jnp.bfloat16

D_MODEL = 2048
SEQ = 2048
HEAD_DIM = 128
A_HEADS = 12
B_HEADS = 8
M_HEADS = 4
M_HEAD_DIM = 256
MEM_LEN = 256
A_WIDTH = A_HEADS * HEAD_DIM
B_WIDTH = B_HEADS * HEAD_DIM
M_WIDTH = M_HEADS * M_HEAD_DIM
N_BRANCH = 3
BLK = 128
NBLK = SEQ // BLK
DILATED_PATTERNS = ((128, 1), (512, 4), (2048, 16))
REL_BUCKETS = 32
REL_MAX_DIST = 2048
EPS = 1e-6
NEG = -1e30
LOG2E = math.log2(math.e)

LANES = 128
VMEM_LIMIT = 56 * 1024 * 1024

COL_QA = 0
COL_KA = COL_QA + A_HEADS
COL_VA = COL_KA + A_HEADS
COL_ZA = COL_VA + A_HEADS
COL_QB = COL_ZA + A_HEADS
COL_KB = COL_QB + B_HEADS
COL_VB = COL_KB + B_HEADS
COL_ZB = COL_VB + B_HEADS
COL_QM = COL_ZB + B_HEADS
COL_ZM = COL_QM + M_WIDTH // LANES
COL_GL = COL_ZM + M_WIDTH // LANES
COL_END = COL_GL + N_BRANCH * D_MODEL // LANES
PROJ_COLS = COL_END * LANES

PROJ_TM = 1024
PROJ_TN = 1024
PROJ_CHUNK = 128
PROJ_NB_HEAD = COL_QM * LANES // PROJ_TN


def _log_sigmoid(x):
    return jnp.minimum(x, 0.0) - jnp.log1p(jnp.exp(-jnp.abs(x)))


def _rms_scale(a):
    return a * lax.rsqrt(jnp.mean(a * a, axis=-1, keepdims=True) + EPS)


def _emit_pipelined(tasks, lookahead):
    pending = []
    for start, finish in tasks:
        pending.append((finish, start()))
        if len(pending) > lookahead:
            fin, state = pending.pop(0)
            fin(*state)
    for fin, state in pending:
        fin(*state)


def _col_block_range(col0, width):
    return col0 * LANES // PROJ_TN, (col0 * LANES + width) // PROJ_TN


def _in_col_blocks(j, *ranges):
    hit = None
    for lo, hi in ranges:
        cond = (j >= lo) & (j < hi)
        hit = cond if hit is None else hit | cond
    return hit


def _proj_body(x_ref, g_ref, wh_ref, wt_ref, gain_ref, wf_ref, bf_ref, out_ref, f_ref, h_ref):
    j = pl.program_id(1)

    @pl.when(j == 0)
    def _():
        hb = (_rms_scale(x_ref[...]) * g_ref[...]).astype(BF16)
        h_ref[...] = hb
        f = jnp.dot(hb, wf_ref[...], preferred_element_type=F32) + bf_ref[...]
        f_ref[...] = _log_sigmoid(f)

    def plain(acc, rows):
        out_ref[rows, :] = acc.astype(BF16)

    def head_normed(hd):
        def epilogue(acc, rows):
            gain = gain_ref[...]
            for s in range(PROJ_TN // hd):
                sl = slice(s * hd, (s + 1) * hd)
                out_ref[rows, sl] = (_rms_scale(acc[:, sl]) * gain[:, sl]).astype(BF16)
        return epilogue

    def run(w_ref, epilogue):
        tasks = []
        for r in range(PROJ_TM // PROJ_CHUNK):
            rows = slice(r * PROJ_CHUNK, (r + 1) * PROJ_CHUNK)
            tasks.append((lambda rows=rows: (jnp.dot(h_ref[rows, :], w_ref[...], preferred_element_type=F32),),
                          lambda acc, rows=rows: epilogue(acc, rows)))
        _emit_pipelined(tasks, 1)

    head128 = _in_col_blocks(j, _col_block_range(COL_QA, 2 * A_WIDTH), _col_block_range(COL_QB, 2 * B_WIDTH))
    head256 = _in_col_blocks(j, _col_block_range(COL_QM, M_WIDTH))
    in_head = j < PROJ_NB_HEAD
    pl.when(head128)(lambda: run(wh_ref, head_normed(HEAD_DIM)))
    pl.when(in_head & jnp.logical_not(head128))(lambda: run(wh_ref, plain))
    pl.when(head256)(lambda: run(wt_ref, head_normed(M_HEAD_DIM)))
    pl.when(jnp.logical_not(in_head | head256))(lambda: run(wt_ref, plain))


def _proj(x2, g, w_head, w_tail, gain, wf, bfp):
    rows = x2.shape[0]
    assert w_tail.shape[1] == PROJ_COLS - PROJ_NB_HEAD * PROJ_TN
    return pl.pallas_call(
        _proj_body,
        grid=(rows // PROJ_TM, PROJ_COLS // PROJ_TN),
        in_specs=[
            pl.BlockSpec((PROJ_TM, D_MODEL), lambda i, j: (i, 0)),
            pl.BlockSpec((1, D_MODEL), lambda i, j: (0, 0)),
            pl.BlockSpec((D_MODEL, PROJ_TN), lambda i, j: (0, jnp.minimum(j, PROJ_NB_HEAD - 1))),
            pl.BlockSpec((D_MODEL, PROJ_TN), lambda i, j: (0, jnp.maximum(j - PROJ_NB_HEAD, 0))),
            pl.BlockSpec((1, PROJ_TN), lambda i, j: (0, j)),
            pl.BlockSpec((D_MODEL, LANES), lambda i, j: (0, 0)),
            pl.BlockSpec((1, LANES), lambda i, j: (0, 0)),
        ],
        out_specs=[
            pl.BlockSpec((PROJ_TM, PROJ_TN), lambda i, j: (i, j)),
            pl.BlockSpec((PROJ_TM, LANES), lambda i, j: (i, 0)),
        ],
        out_shape=[
            jax.ShapeDtypeStruct((rows, PROJ_COLS), BF16),
            jax.ShapeDtypeStruct((rows, LANES), F32),
        ],
        scratch_shapes=[pltpu.VMEM((PROJ_TM, D_MODEL), BF16)],
        compiler_params=pltpu.CompilerParams(
            dimension_semantics=("parallel", "arbitrary"), vmem_limit_bytes=VMEM_LIMIT),
        name="proj",
    )(x2, g, w_head, w_tail, gain, wf, bfp)


def _rel_bucket_np(dist):
    max_exact = REL_BUCKETS // 2
    d_f = np.maximum(dist, 1).astype(np.float32)
    large = max_exact + (np.log(d_f / np.float32(max_exact)) / np.float32(math.log(REL_MAX_DIST / max_exact))
                         * np.float32(REL_BUCKETS - max_exact)).astype(np.int32)
    large = np.minimum(large, REL_BUCKETS - 1)
    return np.where(dist < max_exact, dist, large).astype(np.int32)


def _bias_buckets():
    qi = np.arange(BLK)[:, None]
    kj = np.arange(2 * BLK)[None, :]
    delta = qi - kj + BLK
    tiles = []
    for window, dil in DILATED_PATTERNS:
        w_sub = window // dil
        valid = (delta >= 0) & (delta <= w_sub)
        tiles.append(np.where(valid, _rel_bucket_np(np.clip(delta, 0, w_sub) * dil), -1))
    return np.stack(tiles).astype(np.int32)


def _bias_body(tbl_ref, bucket_ref, out_ref):
    h = pl.program_id(1)
    bucket = bucket_ref[...]
    out = jnp.full(bucket.shape, NEG, F32)
    for r in range(REL_BUCKETS):
        out = jnp.where(bucket == r, tbl_ref[r, h] * LOG2E, out)
    out_ref[...] = out


def _bias_tiles(rel_bias):
    n_pat = len(DILATED_PATTERNS)
    return pl.pallas_call(
        _bias_body,
        grid=(n_pat, A_HEADS),
        in_specs=[pl.BlockSpec(memory_space=pltpu.SMEM),
                  pl.BlockSpec((None, BLK, 2 * BLK), lambda p, h: (p, 0, 0))],
        out_specs=pl.BlockSpec((None, None, BLK, 2 * BLK), lambda p, h: (p, h, 0, 0)),
        out_shape=jax.ShapeDtypeStruct((n_pat, A_HEADS, BLK, 2 * BLK), F32),
        name="rel_bias_tiles",
    )(rel_bias, jnp.asarray(_bias_buckets()))


def _cumsum_body(f_ref, c_ref):
    ft = f_ref[...].T[:B_HEADS, :]
    pos = lax.broadcasted_iota(jnp.int32, ft.shape, 1)
    shift = 1
    while shift < SEQ:
        ft = ft + jnp.where(pos >= shift, pltpu.roll(ft, shift, axis=1), 0.0)
        shift *= 2
    c_ref[...] = ft * LOG2E


def _cumsum(logf, batch):
    return pl.pallas_call(
        _cumsum_body,
        grid=(batch,),
        in_specs=[pl.BlockSpec((SEQ, LANES), lambda b: (b, 0))],
        out_specs=pl.BlockSpec((None, B_HEADS, SEQ), lambda b: (b, 0, 0)),
        out_shape=jax.ShapeDtypeStruct((batch, B_HEADS, SEQ), F32),
        name="forget_cumsum",
    )(logf)


_NT = (((1,), (1,)), ((), ()))


P3_PITCH = BLK + 8


def _scores(q, k, bias):
    return lax.dot_general(q, k, _NT, preferred_element_type=F32) + bias


def _softmax_pv(s, v):
    m = jnp.max(s, axis=-1, keepdims=True)
    pe = jnp.exp2(s - m)
    l = jnp.sum(pe, axis=-1, keepdims=True)
    return jnp.dot(pe.astype(BF16), v, preferred_element_type=F32), m, l


ATTN_A_LOOKAHEAD = 3


def _attn_a_body(q_ref, k_ref, v_ref, bias_ref, o_ref, nat, cm4, res2, res3):
    dil4, dil16 = DILATED_PATTERNS[1][1], DILATED_PATTERNS[2][1]
    cls_len = SEQ // dil4
    for i, ref in enumerate((q_ref, k_ref, v_ref)):
        nat[i] = ref[...].astype(F32)
        for c in range(dil4):
            cm4[i, c * cls_len:(c + 1) * cls_len, :] = nat[i, pl.ds(c, cls_len, stride=dil4), :]

    def operands(q_rows, kv_rows):
        return (cm4[0, q_rows, :].astype(BF16), cm4[1, kv_rows, :].astype(BF16), cm4[2, kv_rows, :].astype(BF16))

    def save(res, rows, acc, m, l):
        res[0, rows, :] = acc
        res[1, rows, :] = jnp.broadcast_to(m, acc.shape)
        res[2, rows, :] = jnp.broadcast_to(l, acc.shape)

    tasks = []

    def add_task(q_rows, kv_rows, bias, res, out_rows):
        def start():
            q, k, v = operands(q_rows, kv_rows)
            return _scores(q, k, bias()), v

        tasks.append((start, lambda s, v: save(res, out_rows, *_softmax_pv(s, v))))

    for c in range(dil4):
        for n in range(cls_len // BLK):
            lo = c * cls_len + n * BLK
            out_rows = pl.ds(n * BLK * dil4 + c, BLK, stride=dil4)
            if n == 0:
                add_task(slice(lo, lo + BLK), slice(lo, lo + BLK), lambda: bias_ref[1, :, BLK:], res2, out_rows)
            else:
                add_task(slice(lo, lo + BLK), slice(lo - BLK, lo + BLK), lambda: bias_ref[1], res2, out_rows)

    for c in range(dil16):
        rows = pl.ds((c % dil4) * cls_len + c // dil4, BLK, stride=dil16 // dil4)
        add_task(rows, rows, lambda: bias_ref[2, :, BLK:], res3, slice(c * P3_PITCH, c * P3_PITCH + BLK))

    def token_order(i, g):
        per_class = BLK // dil16
        return jnp.concatenate(
            [res3[i, pl.ds(g * per_class + j, dil16, stride=P3_PITCH), :] for j in range(per_class)], axis=0)

    def add_merge_task(g):
        rows = slice(g * BLK, (g + 1) * BLK)
        keys = rows if g == 0 else slice((g - 1) * BLK, (g + 1) * BLK)

        def start():
            bias = bias_ref[0, :, BLK:] if g == 0 else bias_ref[0]
            return _scores(q_ref[rows, :], k_ref[keys, :], bias), v_ref[keys, :]

        def finish(s, v):
            acc1, m1, l1 = _softmax_pv(s, v)
            m2, m3 = res2[1, rows, :], token_order(1, g)
            top = jnp.maximum(jnp.maximum(m1, m2), m3)
            w1, w2, w3 = jnp.exp2(m1 - top), jnp.exp2(m2 - top), jnp.exp2(m3 - top)
            num = w1 * acc1 + w2 * res2[0, rows, :] + w3 * token_order(0, g)
            den = w1 * l1 + w2 * res2[2, rows, :] + w3 * token_order(2, g)
            o_ref[rows, :] = (num * (1.0 / den)).astype(BF16)

        tasks.append((start, finish))

    for g in range(NBLK):
        add_merge_task(g)
    _emit_pipelined(tasks, ATTN_A_LOOKAHEAD)


def _attn_a(proj, bias, batch):
    n_pat = len(DILATED_PATTERNS)
    head = lambda col0: pl.BlockSpec((SEQ, HEAD_DIM), lambda b, h: (b, col0 + h))
    p3_rows = DILATED_PATTERNS[2][1] * P3_PITCH
    return pl.pallas_call(
        _attn_a_body,
        grid=(batch, A_HEADS),
        in_specs=[head(COL_QA), head(COL_KA), head(COL_VA),
                  pl.BlockSpec((n_pat, None, BLK, 2 * BLK), lambda b, h: (0, h, 0, 0))],
        out_specs=pl.BlockSpec((SEQ, HEAD_DIM), lambda b, h: (b, h)),
        out_shape=jax.ShapeDtypeStruct((batch * SEQ, A_WIDTH), BF16),
        scratch_shapes=[pltpu.VMEM((3, SEQ, HEAD_DIM), F32)] * 3 + [pltpu.VMEM((3, p3_rows, HEAD_DIM), F32)],
        compiler_params=pltpu.CompilerParams(
            dimension_semantics=("parallel", "parallel"), vmem_limit_bytes=VMEM_LIMIT),
        name="attn_dilated",
    )(proj, proj, proj, bias)


FOX_TQ = 256


def _attn_b_body(q_ref, k_ref, v_ref, c_ref, o_ref):
    h = pl.program_id(1)
    ck = c_ref[pl.ds(h, 1), :]
    qi = lax.broadcasted_iota(jnp.int32, (FOX_TQ, FOX_TQ), 0)
    kj = lax.broadcasted_iota(jnp.int32, (FOX_TQ, FOX_TQ), 1)
    causal = kj <= qi

    def task(i):
        lo, hi = i * FOX_TQ, (i + 1) * FOX_TQ

        def start():
            q = q_ref[lo:hi, :]
            s_diag = lax.dot_general(q, k_ref[lo:hi, :], _NT, preferred_element_type=F32)
            s_off = lax.dot_general(q, k_ref[:lo, :], _NT, preferred_element_type=F32) if i > 0 else None
            return s_diag, s_off

        def finish(s_diag, s_off):
            s_diag = jnp.where(causal, s_diag - ck[:, lo:hi], NEG)
            m = jnp.max(s_diag, axis=-1, keepdims=True)
            if i > 0:
                s_off = s_off - ck[:, :lo]
                m = jnp.maximum(m, jnp.max(s_off, axis=-1, keepdims=True))
            p_diag = jnp.exp2(s_diag - m)
            l = jnp.sum(p_diag, axis=-1, keepdims=True)
            acc = jnp.dot(p_diag.astype(BF16), v_ref[lo:hi, :], preferred_element_type=F32)
            if i > 0:
                p_off = jnp.exp2(s_off - m)
                l = l + jnp.sum(p_off, axis=-1, keepdims=True)
                acc = acc + jnp.dot(p_off.astype(BF16), v_ref[:lo, :], preferred_element_type=F32)
            o_ref[lo:hi, :] = (acc * (1.0 / l)).astype(BF16)

        return start, finish

    _emit_pipelined([task(i) for i in range(SEQ // FOX_TQ)], 1)


def _attn_b(proj, c, batch):
    head = lambda col0: pl.BlockSpec((SEQ, HEAD_DIM), lambda b, h: (b, col0 + h))
    return pl.pallas_call(
        _attn_b_body,
        grid=(batch, B_HEADS),
        in_specs=[head(COL_QB), head(COL_KB), head(COL_VB),
                  pl.BlockSpec((None, B_HEADS, SEQ), lambda b, h: (b, 0, 0))],
        out_specs=pl.BlockSpec((SEQ, HEAD_DIM), lambda b, h: (b, h)),
        out_shape=jax.ShapeDtypeStruct((batch * SEQ, B_WIDTH), BF16),
        compiler_params=pltpu.CompilerParams(
            dimension_semantics=("parallel", "parallel"), vmem_limit_bytes=VMEM_LIMIT),
        name="attn_forgetting",
    )(proj, proj, proj, c)


def _mem_kv_body(mem_ref, g_ref, w_ref, gain_ref, out_ref):
    hb = (_rms_scale(mem_ref[...]) * g_ref[...]).astype(BF16)
    acc = jnp.dot(hb, w_ref[...], preferred_element_type=F32)
    gain = gain_ref[...]
    for s in range(M_HEADS):
        sl = slice(s * M_HEAD_DIM, (s + 1) * M_HEAD_DIM)
        out_ref[:, sl] = (_rms_scale(acc[:, sl]) * gain[:, sl]).astype(BF16)
    out_ref[:, M_WIDTH:] = acc[:, M_WIDTH:].astype(BF16)


def _mem_kv(mem2, g, w, gain, batch):
    return pl.pallas_call(
        _mem_kv_body,
        grid=(batch,),
        in_specs=[
            pl.BlockSpec((MEM_LEN, D_MODEL), lambda b: (b, 0)),
            pl.BlockSpec((1, D_MODEL), lambda b: (0, 0)),
            pl.BlockSpec((D_MODEL, 2 * M_WIDTH), lambda b: (0, 0)),
            pl.BlockSpec((1, M_WIDTH), lambda b: (0, 0)),
        ],
        out_specs=pl.BlockSpec((MEM_LEN, 2 * M_WIDTH), lambda b: (b, 0)),
        out_shape=jax.ShapeDtypeStruct((batch * MEM_LEN, 2 * M_WIDTH), BF16),
        compiler_params=pltpu.CompilerParams(
            dimension_semantics=("parallel",), vmem_limit_bytes=VMEM_LIMIT),
        name="mem_kv",
    )(mem2, g, w, gain)


MEM_TQ = 512


def _attn_m_body(q_ref, k_ref, v_ref, o_ref):
    k = k_ref[...]
    v = v_ref[...]
    for i in range(SEQ // MEM_TQ):
        rows = slice(i * MEM_TQ, (i + 1) * MEM_TQ)
        s = lax.dot_general(q_ref[rows, :], k, _NT, preferred_element_type=F32)
        p = jnp.exp2(s - jnp.max(s, axis=-1, keepdims=True))
        l = jnp.sum(p, axis=-1, keepdims=True)
        acc = jnp.dot(p.astype(BF16), v, preferred_element_type=F32)
        o_ref[rows, :] = (acc * (1.0 / l)).astype(BF16)


def _attn_m(proj, kv, batch):
    col_qm = COL_QM * LANES // M_HEAD_DIM
    return pl.pallas_call(
        _attn_m_body,
        grid=(batch, M_HEADS),
        in_specs=[
            pl.BlockSpec((SEQ, M_HEAD_DIM), lambda b, h: (b, col_qm + h)),
            pl.BlockSpec((MEM_LEN, M_HEAD_DIM), lambda b, h: (b, h)),
            pl.BlockSpec((MEM_LEN, M_HEAD_DIM), lambda b, h: (b, M_HEADS + h)),
        ],
        out_specs=pl.BlockSpec((SEQ, M_HEAD_DIM), lambda b, h: (b, h)),
        out_shape=jax.ShapeDtypeStruct((batch * SEQ, M_WIDTH), BF16),
        compiler_params=pltpu.CompilerParams(
            dimension_semantics=("parallel", "parallel"), vmem_limit_bytes=VMEM_LIMIT),
        name="attn_memory",
    )(proj, kv, kv)


MERGE_TM = 256


def _merge_body(x_ref, oa_ref, za_ref, ob_ref, zb_ref, om_ref, zm_ref, gl_ref, bg_ref,
                wa_ref, wb_ref, wm_ref, wo_ref, out_ref):
    def branch(o_ref, z_ref, w_ref):
        z = z_ref[...].astype(F32)
        gated = o_ref[...].astype(F32) * (z * jax.nn.sigmoid(z))
        return jnp.dot(gated.astype(BF16), w_ref[...], preferred_element_type=F32)

    merged = None
    for n, (o_ref, z_ref, w_ref) in enumerate(((oa_ref, za_ref, wa_ref), (ob_ref, zb_ref, wb_ref),
                                                (om_ref, zm_ref, wm_ref))):
        cols = slice(n * D_MODEL, (n + 1) * D_MODEL)
        gate = jax.nn.sigmoid(gl_ref[:, cols].astype(F32) + bg_ref[:, cols])
        term = gate * branch(o_ref, z_ref, w_ref)
        merged = term if merged is None else merged + term
    out_ref[...] = x_ref[...] + jnp.dot(merged.astype(BF16), wo_ref[...], preferred_element_type=F32)


def _merge(x2, oa, ob, om, proj, bg, wa, wb, wm, wo):
    rows = x2.shape[0]
    row_blk = lambda width, colblk: pl.BlockSpec((MERGE_TM, width), lambda i: (i, colblk))
    resident = lambda shape: pl.BlockSpec(shape, lambda i: (0, 0), pipeline_mode=pl.Buffered(1))
    assert (COL_ZA * LANES) % A_WIDTH == 0 and (COL_ZB * LANES) % B_WIDTH == 0
    assert (COL_ZM * LANES) % M_WIDTH == 0 and (COL_GL * LANES) % (N_BRANCH * D_MODEL) == 0
    return pl.pallas_call(
        _merge_body,
        grid=(rows // MERGE_TM,),
        in_specs=[
            row_blk(D_MODEL, 0),
            row_blk(A_WIDTH, 0), row_blk(A_WIDTH, COL_ZA * LANES // A_WIDTH),
            row_blk(B_WIDTH, 0), row_blk(B_WIDTH, COL_ZB * LANES // B_WIDTH),
            row_blk(M_WIDTH, 0), row_blk(M_WIDTH, COL_ZM * LANES // M_WIDTH),
            row_blk(N_BRANCH * D_MODEL, COL_GL * LANES // (N_BRANCH * D_MODEL)),
            resident((1, N_BRANCH * D_MODEL)),
            resident((A_WIDTH, D_MODEL)), resident((B_WIDTH, D_MODEL)), resident((M_WIDTH, D_MODEL)),
            resident((D_MODEL, D_MODEL)),
        ],
        out_specs=row_blk(D_MODEL, 0),
        out_shape=jax.ShapeDtypeStruct((rows, D_MODEL), F32),
        compiler_params=pltpu.CompilerParams(
            dimension_semantics=("parallel",), vmem_limit_bytes=VMEM_LIMIT),
        name="merge_out",
    )(x2, oa, proj, ob, proj, om, proj, proj, bg, wa, wb, wm, wo)


def _layer(x2, mem2, batch, norm_g, mem_norm_g, w_in, b_forget, b_gate, rel_bias, q_norm_a, k_norm_a,
           q_norm_b, k_norm_b, q_norm_m, k_norm_m, w_mem_kv, w_proj_a, w_proj_b, w_proj_m, w_out):
    f_col = COL_QM * LANES
    w_head = w_in.astype(BF16)
    w_tail = w_head[:, f_col + B_HEADS:]
    w_f = jnp.pad(w_head[:, f_col:f_col + B_HEADS], ((0, 0), (0, LANES - B_HEADS)))
    b_f = jnp.pad(b_forget.astype(F32), (0, LANES - B_HEADS))[None]
    scale = LOG2E / math.sqrt(HEAD_DIM)
    scale_m = LOG2E / math.sqrt(M_HEAD_DIM)
    ones = lambda n: jnp.ones((n,), F32)
    gain = jnp.concatenate([
        jnp.tile(q_norm_a.astype(F32) * scale, A_HEADS), jnp.tile(k_norm_a.astype(F32), A_HEADS), ones(2 * A_WIDTH),
        jnp.tile(q_norm_b.astype(F32) * scale, B_HEADS), jnp.tile(k_norm_b.astype(F32), B_HEADS), ones(2 * B_WIDTH),
        jnp.tile(q_norm_m.astype(F32) * scale_m, M_HEADS), ones(PROJ_COLS - COL_ZM * LANES)])[None]

    proj, logf = _proj(x2, norm_g.astype(F32)[None], w_head, w_tail, gain, w_f, b_f)

    oa = _attn_a(proj, _bias_tiles(rel_bias.astype(F32)), batch)

    c = _cumsum(logf, batch)
    ob = _attn_b(proj, c, batch)

    kv = _mem_kv(mem2, mem_norm_g.astype(F32)[None], w_mem_kv.astype(BF16),
                 jnp.tile(k_norm_m.astype(F32), M_HEADS)[None], batch)
    om = _attn_m(proj, kv, batch)

    return _merge(x2, oa, ob, om, proj, b_gate.astype(F32).reshape(1, N_BRANCH * D_MODEL),
                  w_proj_a.astype(BF16), w_proj_b.astype(BF16), w_proj_m.astype(BF16), w_out.astype(BF16))


def kernel(x, mem, norm_g, mem_norm_g, w_in, b_forget, b_gate, rel_bias, q_norm_a, k_norm_a, q_norm_b, k_norm_b,
           q_norm_m, k_norm_m, w_mem_kv, w_proj_a, w_proj_b, w_proj_m, w_out):
    batch, seq, d_model = x.shape
    assert (seq, d_model) == (SEQ, D_MODEL) and mem.shape == (batch, MEM_LEN, D_MODEL)
    x2 = x.reshape(batch * seq, d_model)
    mem2 = mem.reshape(batch * MEM_LEN, d_model)
    depth = w_in.shape[0]
    for l in range(depth):
        x2 = _layer(x2, mem2, batch, norm_g[l], mem_norm_g[l], w_in[l], b_forget[l], b_gate[l], rel_bias,
                    q_norm_a[l], k_norm_a[l], q_norm_b[l], k_norm_b[l], q_norm_m[l], k_norm_m[l],
                    w_mem_kv[l], w_proj_a[l], w_proj_b[l], w_proj_m[l], w_out[l])
    return x2.reshape(batch, seq, d_model)
```

```python
import functools
import math

import numpy as np
import jax
import jax.numpy as jnp
from jax import lax
from jax.experimental import pallas as pl
from jax.experimental.pallas import tpu as pltpu

F32 = jnp.float32
BF16 = jnp.bfloat16

D_MODEL = 2048
SEQ = 2048
HEAD_DIM = 128
A_HEADS = 12
B_HEADS = 8
M_HEADS = 4
M_HEAD_DIM = 256
MEM_LEN = 256
A_WIDTH = A_HEADS * HEAD_DIM
B_WIDTH = B_HEADS * HEAD_DIM
M_WIDTH = M_HEADS * M_HEAD_DIM
N_BRANCH = 3
BLK = 128
NBLK = SEQ // BLK
DILATED_PATTERNS = ((128, 1), (512, 4), (2048, 16))
REL_BUCKETS = 32
REL_MAX_DIST = 2048
EPS = 1e-6
NEG = -1e30
LOG2E = math.log2(math.e)

LANES = 128
VMEM_LIMIT = 56 * 1024 * 1024

COL_QA = 0
COL_KA = COL_QA + A_HEADS
COL_VA = COL_KA + A_HEADS
COL_ZA = COL_VA + A_HEADS
COL_QB = COL_ZA + A_HEADS
COL_KB = COL_QB + B_HEADS
COL_VB = COL_KB + B_HEADS
COL_ZB = COL_VB + B_HEADS
COL_QM = COL_ZB + B_HEADS
COL_ZM = COL_QM + M_WIDTH // LANES
COL_GL = COL_ZM + M_WIDTH // LANES
COL_END = COL_GL + N_BRANCH * D_MODEL // LANES
PROJ_COLS = COL_END * LANES

PROJ_TM = 1024
PROJ_TN = 1024
PROJ_CHUNK = 128
PROJ_NB_HEAD = COL_QM * LANES // PROJ_TN


def _log_sigmoid(x):
    return jnp.minimum(x, 0.0) - jnp.log1p(jnp.exp(-jnp.abs(x)))


def _rms_scale(a):
    return a * lax.rsqrt(jnp.mean(a * a, axis=-1, keepdims=True) + EPS)


def _emit_pipelined(tasks, lookahead):
    pending = []
    for start, finish in tasks:
        pending.append((finish, start()))
        if len(pending) > lookahead:
            fin, state = pending.pop(0)
            fin(*state)
    for fin, state in pending:
        fin(*state)


def _col_block_range(col0, width):
    return col0 * LANES // PROJ_TN, (col0 * LANES + width) // PROJ_TN


def _in_col_blocks(j, *ranges):
    hit = None
    for lo, hi in ranges:
        cond = (j >= lo) & (j < hi)
        hit = cond if hit is None else hit | cond
    return hit


def _proj_body(x_ref, g_ref, wh_ref, wt_ref, gain_ref, wf_ref, bf_ref, out_ref, f_ref, h_ref):
    j = pl.program_id(1)

    @pl.when(j == 0)
    def _():
        hb = (_rms_scale(x_ref[...]) * g_ref[...]).astype(BF16)
        h_ref[...] = hb
        f = jnp.dot(hb, wf_ref[...], preferred_element_type=F32) + bf_ref[...]
        f_ref[...] = _log_sigmoid(f)

    def plain(acc, rows):
        out_ref[rows, :] = acc.astype(BF16)

    def head_normed(hd):
        def epilogue(acc, rows):
            gain = gain_ref[...]
            for s in range(PROJ_TN // hd):
                sl = slice(s * hd, (s + 1) * hd)
                out_ref[rows, sl] = (_rms_scale(acc[:, sl]) * gain[:, sl]).astype(BF16)
        return epilogue

    def run(w_ref, epilogue):
        tasks = []
        for r in range(PROJ_TM // PROJ_CHUNK):
            rows = slice(r * PROJ_CHUNK, (r + 1) * PROJ_CHUNK)
            tasks.append((lambda rows=rows: (jnp.dot(h_ref[rows, :], w_ref[...], preferred_element_type=F32),),
                          lambda acc, rows=rows: epilogue(acc, rows)))
        _emit_pipelined(tasks, 1)

    head128 = _in_col_blocks(j, _col_block_range(COL_QA, 2 * A_WIDTH), _col_block_range(COL_QB, 2 * B_WIDTH))
    head256 = _in_col_blocks(j, _col_block_range(COL_QM, M_WIDTH))
    in_head = j < PROJ_NB_HEAD
    pl.when(head128)(lambda: run(wh_ref, head_normed(HEAD_DIM)))
    pl.when(in_head & jnp.logical_not(head128))(lambda: run(wh_ref, plain))
    pl.when(head256)(lambda: run(wt_ref, head_normed(M_HEAD_DIM)))
    pl.when(jnp.logical_not(in_head | head256))(lambda: run(wt_ref, plain))


def _proj(x2, g, w_head, w_tail, gain, wf, bfp):
    rows = x2.shape[0]
    assert w_tail.shape[1] == PROJ_COLS - PROJ_NB_HEAD * PROJ_TN
    return pl.pallas_call(
        _proj_body,
        grid=(rows // PROJ_TM, PROJ_COLS // PROJ_TN),
        in_specs=[
            pl.BlockSpec((PROJ_TM, D_MODEL), lambda i, j: (i, 0)),
            pl.BlockSpec((1, D_MODEL), lambda i, j: (0, 0)),
            pl.BlockSpec((D_MODEL, PROJ_TN), lambda i, j: (0, jnp.minimum(j, PROJ_NB_HEAD - 1))),
            pl.BlockSpec((D_MODEL, PROJ_TN), lambda i, j: (0, jnp.maximum(j - PROJ_NB_HEAD, 0))),
            pl.BlockSpec((1, PROJ_TN), lambda i, j: (0, j)),
            pl.BlockSpec((D_MODEL, LANES), lambda i, j: (0, 0)),
            pl.BlockSpec((1, LANES), lambda i, j: (0, 0)),
        ],
        out_specs=[
            pl.BlockSpec((PROJ_TM, PROJ_TN), lambda i, j: (i, j)),
            pl.BlockSpec((PROJ_TM, LANES), lambda i, j: (i, 0)),
        ],
        out_shape=[
            jax.ShapeDtypeStruct((rows, PROJ_COLS), BF16),
            jax.ShapeDtypeStruct((rows, LANES), F32),
        ],
        scratch_shapes=[pltpu.VMEM((PROJ_TM, D_MODEL), BF16)],
        compiler_params=pltpu.CompilerParams(
            dimension_semantics=("parallel", "arbitrary"), vmem_limit_bytes=VMEM_LIMIT),
        name="proj",
    )(x2, g, w_head, w_tail, gain, wf, bfp)


def _rel_bucket_np(dist):
    max_exact = REL_BUCKETS // 2
    d_f = np.maximum(dist, 1).astype(np.float32)
    large = max_exact + (np.log(d_f / np.float32(max_exact)) / np.float32(math.log(REL_MAX_DIST / max_exact))
                         * np.float32(REL_BUCKETS - max_exact)).astype(np.int32)
    large = np.minimum(large, REL_BUCKETS - 1)
    return np.where(dist < max_exact, dist, large).astype(np.int32)


def _bias_buckets():
    qi = np.arange(BLK)[:, None]
    kj = np.arange(2 * BLK)[None, :]
    delta = qi - kj + BLK
    tiles = []
    for window, dil in DILATED_PATTERNS:
        w_sub = window // dil
        valid = (delta >= 0) & (delta <= w_sub)
        tiles.append(np.where(valid, _rel_bucket_np(np.clip(delta, 0, w_sub) * dil), -1))
    return np.stack(tiles).astype(np.int32)


def _bias_body(tbl_ref, bucket_ref, out_ref):
    h = pl.program_id(1)
    bucket = bucket_ref[...]
    out = jnp.full(bucket.shape, NEG, F32)
    for r in range(REL_BUCKETS):
        out = jnp.where(bucket == r, tbl_ref[r, h] * LOG2E, out)
    out_ref[...] = out


def _bias_tiles(rel_bias):
    n_pat = len(DILATED_PATTERNS)
    return pl.pallas_call(
        _bias_body,
        grid=(n_pat, A_HEADS),
        in_specs=[pl.BlockSpec(memory_space=pltpu.SMEM),
                  pl.BlockSpec((None, BLK, 2 * BLK), lambda p, h: (p, 0, 0))],
        out_specs=pl.BlockSpec((None, None, BLK, 2 * BLK), lambda p, h: (p, h, 0, 0)),
        out_shape=jax.ShapeDtypeStruct((n_pat, A_HEADS, BLK, 2 * BLK), F32),
        name="rel_bias_tiles",
    )(rel_bias, jnp.asarray(_bias_buckets()))


def _cumsum_body(f_ref, c_ref, ccol_ref):
    ft = f_ref[...].T[:B_HEADS, :]
    pos = lax.broadcasted_iota(jnp.int32, ft.shape, 1)
    shift = 1
    while shift < SEQ:
        ft = ft + jnp.where(pos >= shift, pltpu.roll(ft, shift, axis=1), 0.0)
        shift *= 2
    c = ft * LOG2E
    c_ref[...] = c
    ccol_ref[...] = jnp.concatenate([c, jnp.zeros((LANES - B_HEADS, SEQ), F32)], axis=0).T


def _cumsum(logf, batch):
    return pl.pallas_call(
        _cumsum_body,
        grid=(batch,),
        in_specs=[pl.BlockSpec((SEQ, LANES), lambda b: (b, 0))],
        out_specs=[pl.BlockSpec((None, B_HEADS, SEQ), lambda b: (b, 0, 0)),
                   pl.BlockSpec((SEQ, LANES), lambda b: (b, 0))],
        out_shape=[jax.ShapeDtypeStruct((batch, B_HEADS, SEQ), F32),
                   jax.ShapeDtypeStruct((batch * SEQ, LANES), F32)],
        name="forget_cumsum",
    )(logf)


_NT = (((1,), (1,)), ((), ()))


P3_PITCH = BLK + 8


def _scores(q, k, bias):
    return lax.dot_general(q, k, _NT, preferred_element_type=F32) + bias


def _softmax_pv(s, v, bounded):
    m = 0.0 if bounded else jnp.max(s, axis=-1, keepdims=True)
    pe = jnp.exp2(s) if bounded else jnp.exp2(s - m)
    l = jnp.sum(pe, axis=-1, keepdims=True)
    return jnp.dot(pe.astype(BF16), v, preferred_element_type=F32), l, m


LOGIT_RANGE = 60.0
ATTN_A_LOOKAHEAD = 8


def _attn_a_body(bounded, q_ref, k_ref, v_ref, bias_ref, o_ref, nat, cm4, res2, res3):
    dil4, dil16 = DILATED_PATTERNS[1][1], DILATED_PATTERNS[2][1]
    cls_len = SEQ // dil4
    for i, ref in enumerate((q_ref, k_ref, v_ref)):
        nat[i] = ref[...].astype(F32)
        for c in range(dil4):
            cm4[i, c * cls_len:(c + 1) * cls_len, :] = nat[i, pl.ds(c, cls_len, stride=dil4), :]

    def operands(q_rows, kv_rows):
        return (cm4[0, q_rows, :].astype(BF16), cm4[1, kv_rows, :].astype(BF16), cm4[2, kv_rows, :].astype(BF16))

    def save(res, rows, acc, l, m):
        res[0, rows, :] = acc
        res[1, rows, :] = jnp.broadcast_to(l, acc.shape)
        if not bounded:
            res[2, rows, :] = jnp.broadcast_to(m, acc.shape)

    tasks = []

    def add_task(q_rows, kv_rows, bias, res, out_rows):
        def start():
            q, k, v = operands(q_rows, kv_rows)
            return _scores(q, k, bias()), v

        tasks.append((start, lambda s, v: save(res, out_rows, *_softmax_pv(s, v, bounded))))

    for c in range(dil4):
        for n in range(cls_len // BLK):
            lo = c * cls_len + n * BLK
            out_rows = pl.ds(n * BLK * dil4 + c, BLK, stride=dil4)
            if n == 0:
                add_task(slice(lo, lo + BLK), slice(lo, lo + BLK), lambda: bias_ref[1, :, BLK:], res2, out_rows)
            else:
                add_task(slice(lo, lo + BLK), slice(lo - BLK, lo + BLK), lambda: bias_ref[1], res2, out_rows)

    for c in range(dil16):
        rows = pl.ds((c % dil4) * cls_len + c // dil4, BLK, stride=dil16 // dil4)
        add_task(rows, rows, lambda: bias_ref[2, :, BLK:], res3, slice(c * P3_PITCH, c * P3_PITCH + BLK))

    def token_order(i, g):
        per_class = BLK // dil16
        return jnp.concatenate(
            [res3[i, pl.ds(g * per_class + j, dil16, stride=P3_PITCH), :] for j in range(per_class)], axis=0)

    def add_merge_task(g):
        rows = slice(g * BLK, (g + 1) * BLK)
        keys = rows if g == 0 else slice((g - 1) * BLK, (g + 1) * BLK)

        def start():
            bias = bias_ref[0, :, BLK:] if g == 0 else bias_ref[0]
            return _scores(q_ref[rows, :], k_ref[keys, :], bias), v_ref[keys, :]

        def finish(s, v):
            acc1, l1, m1 = _softmax_pv(s, v, bounded)
            if bounded:
                num = acc1 + res2[0, rows, :] + token_order(0, g)
                den = l1 + res2[1, rows, :] + token_order(1, g)
            else:
                m2, m3 = res2[2, rows, :], token_order(2, g)
                top = jnp.maximum(jnp.maximum(m1, m2), m3)
                w1, w2, w3 = jnp.exp2(m1 - top), jnp.exp2(m2 - top), jnp.exp2(m3 - top)
                num = w1 * acc1 + w2 * res2[0, rows, :] + w3 * token_order(0, g)
                den = w1 * l1 + w2 * res2[1, rows, :] + w3 * token_order(1, g)
            o_ref[rows, :] = (num * (1.0 / den)).astype(BF16)

        tasks.append((start, finish))

    for g in range(NBLK):
        add_merge_task(g)
    _emit_pipelined(tasks, ATTN_A_LOOKAHEAD)


def _attn_a(proj, bias, batch, bounded):
    n_pat = len(DILATED_PATTERNS)
    head = lambda col0: pl.BlockSpec((SEQ, HEAD_DIM), lambda b, h: (b, col0 + h))
    p3_rows = DILATED_PATTERNS[2][1] * P3_PITCH
    n_stats = 2 if bounded else 3
    return pl.pallas_call(
        functools.partial(_attn_a_body, bounded),
        grid=(batch, A_HEADS),
        in_specs=[head(COL_QA), head(COL_KA), head(COL_VA),
                  pl.BlockSpec((n_pat, None, BLK, 2 * BLK), lambda b, h: (0, h, 0, 0))],
        out_specs=pl.BlockSpec((SEQ, HEAD_DIM), lambda b, h: (b, h)),
        out_shape=jax.ShapeDtypeStruct((batch * SEQ, A_WIDTH), BF16),
        scratch_shapes=[pltpu.VMEM((3, SEQ, HEAD_DIM), F32)] * 2 + [
            pltpu.VMEM((n_stats, SEQ, HEAD_DIM), F32), pltpu.VMEM((n_stats, p3_rows, HEAD_DIM), F32)],
        compiler_params=pltpu.CompilerParams(
            dimension_semantics=("parallel", "parallel"), vmem_limit_bytes=VMEM_LIMIT),
        name="attn_dilated_bounded" if bounded else "attn_dilated",
    )(proj, proj, proj, bias)


FOX_TQ = 256


def _attn_b_body(bounded, q_ref, k_ref, v_ref, c_ref, ccol_ref, o_ref):
    h = pl.program_id(1)
    ck = c_ref[pl.ds(h, 1), :]
    qi = lax.broadcasted_iota(jnp.int32, (FOX_TQ, FOX_TQ), 0)
    kj = lax.broadcasted_iota(jnp.int32, (FOX_TQ, FOX_TQ), 1)
    causal = kj <= qi
    head_lane = lax.broadcasted_iota(jnp.int32, (FOX_TQ, LANES), 1) == h

    def task(i):
        lo, hi = i * FOX_TQ, (i + 1) * FOX_TQ

        def start():
            q = q_ref[lo:hi, :]
            s_diag = lax.dot_general(q, k_ref[lo:hi, :], _NT, preferred_element_type=F32)
            s_off = lax.dot_general(q, k_ref[:lo, :], _NT, preferred_element_type=F32) if i > 0 else None
            return s_diag, s_off

        def finish(s_diag, s_off):
            if bounded:
                cq = jnp.sum(jnp.where(head_lane, ccol_ref[lo:hi, :], 0.0), axis=-1, keepdims=True)
                p_diag = jnp.exp2(jnp.where(causal, (s_diag + cq) - ck[:, lo:hi], NEG))
                p_off = jnp.exp2((s_off + cq) - ck[:, :lo]) if i > 0 else None
            else:
                s_diag = jnp.where(causal, s_diag - ck[:, lo:hi], NEG)
                m = jnp.max(s_diag, axis=-1, keepdims=True)
                if i > 0:
                    s_off = s_off - ck[:, :lo]
                    m = jnp.maximum(m, jnp.max(s_off, axis=-1, keepdims=True))
                p_diag = jnp.exp2(s_diag - m)
                p_off = jnp.exp2(s_off - m) if i > 0 else None
            l = jnp.sum(p_diag, axis=-1, keepdims=True)
            acc = jnp.dot(p_diag.astype(BF16), v_ref[lo:hi, :], preferred_element_type=F32)
            if i > 0:
                l = l + jnp.sum(p_off, axis=-1, keepdims=True)
                acc = acc + jnp.dot(p_off.astype(BF16), v_ref[:lo, :], preferred_element_type=F32)
            o_ref[lo:hi, :] = (acc * (1.0 / l)).astype(BF16)

        return start, finish

    _emit_pipelined([task(i) for i in range(SEQ // FOX_TQ)], 2)


def _attn_b(proj, c, ccol, batch, bounded):
    head = lambda col0: pl.BlockSpec((SEQ, HEAD_DIM), lambda b, h: (b, col0 + h))
    return pl.pallas_call(
        functools.partial(_attn_b_body, bounded),
        grid=(batch, B_HEADS),
        in_specs=[head(COL_QB), head(COL_KB), head(COL_VB),
                  pl.BlockSpec((None, B_HEADS, SEQ), lambda b, h: (b, 0, 0)),
                  pl.BlockSpec((SEQ, LANES), lambda b, h: (b, 0))],
        out_specs=pl.BlockSpec((SEQ, HEAD_DIM), lambda b, h: (b, h)),
        out_shape=jax.ShapeDtypeStruct((batch * SEQ, B_WIDTH), BF16),
        compiler_params=pltpu.CompilerParams(
            dimension_semantics=("parallel", "parallel"), vmem_limit_bytes=VMEM_LIMIT),
        name="attn_forgetting_bounded" if bounded else "attn_forgetting",
    )(proj, proj, proj, c, ccol)


def _mem_kv_body(mem_ref, g_ref, w_ref, gain_ref, out_ref):
    hb = (_rms_scale(mem_ref[...]) * g_ref[...]).astype(BF16)
    acc = jnp.dot(hb, w_ref[...], preferred_element_type=F32)
    gain = gain_ref[...]
    for s in range(M_HEADS):
        sl = slice(s * M_HEAD_DIM, (s + 1) * M_HEAD_DIM)
        out_ref[:, sl] = (_rms_scale(acc[:, sl]) * gain[:, sl]).astype(BF16)
    out_ref[:, M_WIDTH:] = acc[:, M_WIDTH:].astype(BF16)


def _mem_kv(mem2, g, w, gain, batch):
    return pl.pallas_call(
        _mem_kv_body,
        grid=(batch,),
        in_specs=[
            pl.BlockSpec((MEM_LEN, D_MODEL), lambda b: (b, 0)),
            pl.BlockSpec((1, D_MODEL), lambda b: (0, 0)),
            pl.BlockSpec((D_MODEL, 2 * M_WIDTH), lambda b: (0, 0)),
            pl.BlockSpec((1, M_WIDTH), lambda b: (0, 0)),
        ],
        out_specs=pl.BlockSpec((MEM_LEN, 2 * M_WIDTH), lambda b: (b, 0)),
        out_shape=jax.ShapeDtypeStruct((batch * MEM_LEN, 2 * M_WIDTH), BF16),
        compiler_params=pltpu.CompilerParams(
            dimension_semantics=("parallel",), vmem_limit_bytes=VMEM_LIMIT),
        name="mem_kv",
    )(mem2, g, w, gain)


MEM_TQ = 512


def _attn_m_body(bounded, q_ref, k_ref, v_ref, o_ref):
    def task(i):
        rows = slice(i * MEM_TQ, (i + 1) * MEM_TQ)

        def finish(s):
            acc, l, _ = _softmax_pv(s, v_ref[...], bounded)
            o_ref[rows, :] = (acc * (1.0 / l)).astype(BF16)

        return lambda: (lax.dot_general(q_ref[rows, :], k_ref[...], _NT, preferred_element_type=F32),), finish

    _emit_pipelined([task(i) for i in range(SEQ // MEM_TQ)], 1)


def _attn_m(proj, kv, batch, bounded):
    col_qm = COL_QM * LANES // M_HEAD_DIM
    return pl.pallas_call(
        functools.partial(_attn_m_body, bounded),
        grid=(batch, M_HEADS),
        in_specs=[
            pl.BlockSpec((SEQ, M_HEAD_DIM), lambda b, h: (b, col_qm + h)),
            pl.BlockSpec((MEM_LEN, M_HEAD_DIM), lambda b, h: (b, h)),
            pl.BlockSpec((MEM_LEN, M_HEAD_DIM), lambda b, h: (b, M_HEADS + h)),
        ],
        out_specs=pl.BlockSpec((SEQ, M_HEAD_DIM), lambda b, h: (b, h)),
        out_shape=jax.ShapeDtypeStruct((batch * SEQ, M_WIDTH), BF16),
        compiler_params=pltpu.CompilerParams(
            dimension_semantics=("parallel", "parallel"), vmem_limit_bytes=VMEM_LIMIT),
        name="attn_memory_bounded" if bounded else "attn_memory",
    )(proj, kv, kv)


MERGE_TM = 256


def _merge_body(x_ref, oa_ref, za_ref, ob_ref, zb_ref, om_ref, zm_ref, gl_ref, bg_ref,
                wa_ref, wb_ref, wm_ref, wo_ref, out_ref):
    def branch(o_ref, z_ref, w_ref):
        z = z_ref[...].astype(F32)
        gated = o_ref[...].astype(F32) * (z * jax.nn.sigmoid(z))
        return jnp.dot(gated.astype(BF16), w_ref[...], preferred_element_type=F32)

    merged = None
    for n, (o_ref, z_ref, w_ref) in enumerate(((oa_ref, za_ref, wa_ref), (ob_ref, zb_ref, wb_ref),
                                                (om_ref, zm_ref, wm_ref))):
        cols = slice(n * D_MODEL, (n + 1) * D_MODEL)
        gate = jax.nn.sigmoid(gl_ref[:, cols].astype(F32) + bg_ref[:, cols])
        term = gate * branch(o_ref, z_ref, w_ref)
        merged = term if merged is None else merged + term
    out_ref[...] = x_ref[...] + jnp.dot(merged.astype(BF16), wo_ref[...], preferred_element_type=F32)


def _merge(x2, oa, ob, om, proj, bg, wa, wb, wm, wo):
    rows = x2.shape[0]
    row_blk = lambda width, colblk: pl.BlockSpec((MERGE_TM, width), lambda i: (i, colblk))
    resident = lambda shape: pl.BlockSpec(shape, lambda i: (0, 0), pipeline_mode=pl.Buffered(1))
    assert (COL_ZA * LANES) % A_WIDTH == 0 and (COL_ZB * LANES) % B_WIDTH == 0
    assert (COL_ZM * LANES) % M_WIDTH == 0 and (COL_GL * LANES) % (N_BRANCH * D_MODEL) == 0
    return pl.pallas_call(
        _merge_body,
        grid=(rows // MERGE_TM,),
        in_specs=[
            row_blk(D_MODEL, 0),
            row_blk(A_WIDTH, 0), row_blk(A_WIDTH, COL_ZA * LANES // A_WIDTH),
            row_blk(B_WIDTH, 0), row_blk(B_WIDTH, COL_ZB * LANES // B_WIDTH),
            row_blk(M_WIDTH, 0), row_blk(M_WIDTH, COL_ZM * LANES // M_WIDTH),
            row_blk(N_BRANCH * D_MODEL, COL_GL * LANES // (N_BRANCH * D_MODEL)),
            resident((1, N_BRANCH * D_MODEL)),
            resident((A_WIDTH, D_MODEL)), resident((B_WIDTH, D_MODEL)), resident((M_WIDTH, D_MODEL)),
            resident((D_MODEL, D_MODEL)),
        ],
        out_specs=row_blk(D_MODEL, 0),
        out_shape=jax.ShapeDtypeStruct((rows, D_MODEL), F32),
        compiler_params=pltpu.CompilerParams(
            dimension_semantics=("parallel",), vmem_limit_bytes=VMEM_LIMIT),
        name="merge_out",
    )(x2, oa, proj, ob, proj, om, proj, proj, bg, wa, wb, wm, wo)


def _layer(x2, mem2, batch, norm_g, mem_norm_g, w_in, b_forget, b_gate, rel_bias, q_norm_a, k_norm_a,
           q_norm_b, k_norm_b, q_norm_m, k_norm_m, w_mem_kv, w_proj_a, w_proj_b, w_proj_m, w_out):
    f_col = COL_QM * LANES
    w_head = w_in.astype(BF16)
    w_tail = w_head[:, f_col + B_HEADS:]
    w_f = jnp.pad(w_head[:, f_col:f_col + B_HEADS], ((0, 0), (0, LANES - B_HEADS)))
    b_f = jnp.pad(b_forget.astype(F32), (0, LANES - B_HEADS))[None]
    scale = LOG2E / math.sqrt(HEAD_DIM)
    scale_m = LOG2E / math.sqrt(M_HEAD_DIM)
    ones = lambda n: jnp.ones((n,), F32)
    gain = jnp.concatenate([
        jnp.tile(q_norm_a.astype(F32) * scale, A_HEADS), jnp.tile(k_norm_a.astype(F32), A_HEADS), ones(2 * A_WIDTH),
        jnp.tile(q_norm_b.astype(F32) * scale, B_HEADS), jnp.tile(k_norm_b.astype(F32), B_HEADS), ones(2 * B_WIDTH),
        jnp.tile(q_norm_m.astype(F32) * scale_m, M_HEADS), ones(PROJ_COLS - COL_ZM * LANES)])[None]

    proj, logf = _proj(x2, norm_g.astype(F32)[None], w_head, w_tail, gain, w_f, b_f)

    def logit_bound(gq, gk, dim, scl):
        return dim * scl * jnp.max(jnp.abs(gq.astype(F32))) * jnp.max(jnp.abs(gk.astype(F32)))

    def dispatch(bound, call, *operands):
        return lax.cond(bound <= LOGIT_RANGE, functools.partial(call, bounded=True),
                        functools.partial(call, bounded=False), *operands)

    bound_a = logit_bound(q_norm_a, k_norm_a, HEAD_DIM, scale) + LOG2E * jnp.max(jnp.abs(rel_bias.astype(F32)))
    oa = dispatch(bound_a, lambda p, b, bounded: _attn_a(p, b, batch, bounded),
                  proj, _bias_tiles(rel_bias.astype(F32)))

    c, ccol = _cumsum(logf, batch)
    ob = dispatch(logit_bound(q_norm_b, k_norm_b, HEAD_DIM, scale),
                  lambda p, c_, cc, bounded: _attn_b(p, c_, cc, batch, bounded), proj, c, ccol)

    kv = _mem_kv(mem2, mem_norm_g.astype(F32)[None], w_mem_kv.astype(BF16),
                 jnp.tile(k_norm_m.astype(F32), M_HEADS)[None], batch)
    om = dispatch(logit_bound(q_norm_m, k_norm_m, M_HEAD_DIM, scale_m),
                  lambda p, kv_, bounded: _attn_m(p, kv_, batch, bounded), proj, kv)

    return _merge(x2, oa, ob, om, proj, b_gate.astype(F32).reshape(1, N_BRANCH * D_MODEL),
                  w_proj_a.astype(BF16), w_proj_b.astype(BF16), w_proj_m.astype(BF16), w_out.astype(BF16))


def kernel(x, mem, norm_g, mem_norm_g, w_in, b_forget, b_gate, rel_bias, q_norm_a, k_norm_a, q_norm_b, k_norm_b,
           q_norm_m, k_norm_m, w_mem_kv, w_proj_a, w_proj_b, w_proj_m, w_out):
    batch, seq, d_model = x.shape
    assert (seq, d_model) == (SEQ, D_MODEL) and mem.shape == (batch, MEM_LEN, D_MODEL)
    x2 = x.reshape(batch * seq, d_model)
    mem2 = mem.reshape(batch * MEM_LEN, d_model)
    depth = w_in.shape[0]
    for l in range(depth):
        x2 = _layer(x2, mem2, batch, norm_g[l], mem_norm_g[l], w_in[l], b_forget[l], b_gate[l], rel_bias,
                    q_norm_a[l], k_norm_a[l], q_norm_b[l], k_norm_b[l], q_norm_m[l], k_norm_m[l],
                    w_mem_kv[l], w_proj_a[l], w_proj_b[l], w_proj_m[l], w_out[l])
    return x2.reshape(batch, seq, d_model)
```

```python
import functools
import math

import numpy as np
import jax
import jax.numpy as jnp
from jax import lax
from jax.experimental import pallas as pl
from jax.experimental.pallas import tpu as pltpu

F32 = jnp.float32
BF16 = jnp.bfloat16

D_MODEL = 2048
SEQ = 2048
HEAD_DIM = 128
A_HEADS = 12
B_HEADS = 8
M_HEADS = 4
M_HEAD_DIM = 256
MEM_LEN = 256
A_WIDTH = A_HEADS * HEAD_DIM
B_WIDTH = B_HEADS * HEAD_DIM
M_WIDTH = M_HEADS * M_HEAD_DIM
N_BRANCH = 3
BLK = 128
NBLK = SEQ // BLK
DILATED_PATTERNS = ((128, 1), (512, 4), (2048, 16))
REL_BUCKETS = 32
REL_MAX_DIST = 2048
EPS = 1e-6
NEG = -1e30
LOG2E = math.log2(math.e)

LANES = 128
VMEM_LIMIT = 56 * 1024 * 1024

COL_QA = 0
COL_KA = COL_QA + A_HEADS
COL_VA = COL_KA + A_HEADS
COL_ZA = COL_VA + A_HEADS
COL_QB = COL_ZA + A_HEADS
COL_KB = COL_QB + B_HEADS
COL_VB = COL_KB + B_HEADS
COL_ZB = COL_VB + B_HEADS
COL_QM = COL_ZB + B_HEADS
COL_ZM = COL_QM + M_WIDTH // LANES
COL_GL = COL_ZM + M_WIDTH // LANES
COL_END = COL_GL + N_BRANCH * D_MODEL // LANES
PROJ_COLS = COL_END * LANES

NORM_TM = 1024
PROJ_TM = 2048
PROJ_TN = 1024
PROJ_CHUNK = 128
PROJ_NB_HEAD = COL_QM * LANES // PROJ_TN
F_SHIFT = B_HEADS


def _log_sigmoid(x):
    return jnp.minimum(x, 0.0) - jnp.log1p(jnp.exp(-jnp.abs(x)))


def _rms_scale(a):
    return a * lax.rsqrt(jnp.mean(a * a, axis=-1, keepdims=True) + EPS)


def _emit_pipelined(tasks, lookahead):
    pending = []
    for start, finish in tasks:
        pending.append((finish, start()))
        if len(pending) > lookahead:
            fin, state = pending.pop(0)
            fin(*state)
    for fin, state in pending:
        fin(*state)


def _col_block_range(col0, width):
    return col0 * LANES // PROJ_TN, (col0 * LANES + width) // PROJ_TN


def _in_col_blocks(j, *ranges):
    hit = None
    for lo, hi in ranges:
        cond = (j >= lo) & (j < hi)
        hit = cond if hit is None else hit | cond
    return hit


def _norm_body(x_ref, g_ref, wf_ref, bf_ref, h_ref, f_ref):
    hb = (_rms_scale(x_ref[...]) * g_ref[...]).astype(BF16)
    h_ref[...] = hb
    f = jnp.dot(hb, wf_ref[...], preferred_element_type=F32) + bf_ref[...]
    f_ref[...] = _log_sigmoid(f)


def _norm(x2, g, wf, bfp):
    rows = x2.shape[0]
    return pl.pallas_call(
        _norm_body,
        grid=(rows // NORM_TM,),
        in_specs=[
            pl.BlockSpec((NORM_TM, D_MODEL), lambda i: (i, 0)),
            pl.BlockSpec((1, D_MODEL), lambda i: (0, 0)),
            pl.BlockSpec((D_MODEL, LANES), lambda i: (0, 0)),
            pl.BlockSpec((1, LANES), lambda i: (0, 0)),
        ],
        out_specs=[pl.BlockSpec((NORM_TM, D_MODEL), lambda i: (i, 0)),
                   pl.BlockSpec((NORM_TM, LANES), lambda i: (i, 0))],
        out_shape=[jax.ShapeDtypeStruct((rows, D_MODEL), BF16), jax.ShapeDtypeStruct((rows, LANES), F32)],
        compiler_params=pltpu.CompilerParams(dimension_semantics=("parallel",), vmem_limit_bytes=VMEM_LIMIT),
        name="norm",
    )(x2, g, wf, bfp)


def _proj_body(h_ref, wa_ref, wb_ref, gain_ref, out_ref, w16_ref):
    j = pl.program_id(0)

    @pl.when((pl.program_id(1) == 0) & (j < PROJ_NB_HEAD))
    def _():
        w16_ref[...] = wa_ref[...].astype(BF16)

    @pl.when((pl.program_id(1) == 0) & (j >= PROJ_NB_HEAD))
    def _():
        wide = jnp.concatenate([wa_ref[...], wb_ref[...]], axis=1)
        w16_ref[...] = pltpu.roll(wide, wide.shape[1] - F_SHIFT, axis=1)[:, :PROJ_TN].astype(BF16)

    def plain(acc, rows):
        out_ref[rows, :] = acc.astype(BF16)

    def head_normed(hd):
        def epilogue(acc, rows):
            gain = gain_ref[...]
            for s in range(PROJ_TN // hd):
                sl = slice(s * hd, (s + 1) * hd)
                out_ref[rows, sl] = (_rms_scale(acc[:, sl]) * gain[:, sl]).astype(BF16)
        return epilogue

    def run(epilogue):
        tasks = []
        for r in range(PROJ_TM // PROJ_CHUNK):
            rows = slice(r * PROJ_CHUNK, (r + 1) * PROJ_CHUNK)
            tasks.append((lambda rows=rows: (jnp.dot(h_ref[rows, :], w16_ref[...], preferred_element_type=F32),),
                          lambda acc, rows=rows: epilogue(acc, rows)))
        _emit_pipelined(tasks, 1)

    head128 = _in_col_blocks(j, _col_block_range(COL_QA, 2 * A_WIDTH), _col_block_range(COL_QB, 2 * B_WIDTH))
    head256 = _in_col_blocks(j, _col_block_range(COL_QM, M_WIDTH))
    pl.when(head128)(lambda: run(head_normed(HEAD_DIM)))
    pl.when(head256)(lambda: run(head_normed(M_HEAD_DIM)))
    pl.when(jnp.logical_not(head128 | head256))(lambda: run(plain))


def _proj(h, w_in, gain):
    rows = h.shape[0]
    lanes_per_block = PROJ_TN // LANES

    def w_main(j, i):
        return 0, j

    def w_spill(j, i):
        return 0, (jnp.maximum(j, PROJ_NB_HEAD) + 1) * lanes_per_block

    return pl.pallas_call(
        _proj_body,
        grid=(PROJ_COLS // PROJ_TN, rows // PROJ_TM),
        in_specs=[
            pl.BlockSpec((PROJ_TM, D_MODEL), lambda j, i: (i, 0)),
            pl.BlockSpec((D_MODEL, PROJ_TN), w_main),
            pl.BlockSpec((D_MODEL, LANES), w_spill),
            pl.BlockSpec((1, PROJ_TN), lambda j, i: (0, j)),
        ],
        out_specs=pl.BlockSpec((PROJ_TM, PROJ_TN), lambda j, i: (i, j)),
        out_shape=jax.ShapeDtypeStruct((rows, PROJ_COLS), BF16),
        scratch_shapes=[pltpu.VMEM((D_MODEL, PROJ_TN), BF16)],
        compiler_params=pltpu.CompilerParams(
            dimension_semantics=("parallel", "arbitrary"), vmem_limit_bytes=VMEM_LIMIT),
        name="proj",
    )(h, w_in, w_in, gain)


def _rel_bucket_np(dist):
    max_exact = REL_BUCKETS // 2
    d_f = np.maximum(dist, 1).astype(np.float32)
    large = max_exact + (np.log(d_f / np.float32(max_exact)) / np.float32(math.log(REL_MAX_DIST / max_exact))
                         * np.float32(REL_BUCKETS - max_exact)).astype(np.int32)
    large = np.minimum(large, REL_BUCKETS - 1)
    return np.where(dist < max_exact, dist, large).astype(np.int32)


def _bias_buckets():
    qi = np.arange(BLK)[:, None]
    kj = np.arange(2 * BLK)[None, :]
    delta = qi - kj + BLK
    tiles = []
    for window, dil in DILATED_PATTERNS:
        w_sub = window // dil
        valid = (delta >= 0) & (delta <= w_sub)
        tiles.append(np.where(valid, _rel_bucket_np(np.clip(delta, 0, w_sub) * dil), -1))
    return np.stack(tiles).astype(np.int32)


def _bias_body(tbl_ref, bucket_ref, out_ref):
    h = pl.program_id(1)
    bucket = bucket_ref[...]
    out = jnp.full(bucket.shape, NEG, F32)
    for r in range(REL_BUCKETS):
        out = jnp.where(bucket == r, tbl_ref[r, h] * LOG2E, out)
    out_ref[...] = out


def _bias_tiles(rel_bias):
    n_pat = len(DILATED_PATTERNS)
    return pl.pallas_call(
        _bias_body,
        grid=(n_pat, A_HEADS),
        in_specs=[pl.BlockSpec(memory_space=pltpu.SMEM),
                  pl.BlockSpec((None, BLK, 2 * BLK), lambda p, h: (p, 0, 0))],
        out_specs=pl.BlockSpec((None, None, BLK, 2 * BLK), lambda p, h: (p, h, 0, 0)),
        out_shape=jax.ShapeDtypeStruct((n_pat, A_HEADS, BLK, 2 * BLK), F32),
        name="rel_bias_tiles",
    )(rel_bias, jnp.asarray(_bias_buckets()))


def _cumsum_body(f_ref, c_ref, ccol_ref):
    ft = f_ref[...].T[:B_HEADS, :]
    pos = lax.broadcasted_iota(jnp.int32, ft.shape, 1)
    shift = 1
    while shift < SEQ:
        ft = ft + jnp.where(pos >= shift, pltpu.roll(ft, shift, axis=1), 0.0)
        shift *= 2
    c = ft * LOG2E
    c_ref[...] = c
    ccol_ref[...] = jnp.concatenate([c, jnp.zeros((LANES - B_HEADS, SEQ), F32)], axis=0).T


def _cumsum(logf, batch):
    return pl.pallas_call(
        _cumsum_body,
        grid=(batch,),
        in_specs=[pl.BlockSpec((SEQ, LANES), lambda b: (b, 0))],
        out_specs=[pl.BlockSpec((None, B_HEADS, SEQ), lambda b: (b, 0, 0)),
                   pl.BlockSpec((SEQ, LANES), lambda b: (b, 0))],
        out_shape=[jax.ShapeDtypeStruct((batch, B_HEADS, SEQ), F32),
                   jax.ShapeDtypeStruct((batch * SEQ, LANES), F32)],
        name="forget_cumsum",
    )(logf)


_NT = (((1,), (1,)), ((), ()))


P3_PITCH = BLK + 8


def _scores(q, k, bias):
    return lax.dot_general(q, k, _NT, preferred_element_type=F32) + bias


def _softmax_pv(s, v, bounded):
    m = 0.0 if bounded else jnp.max(s, axis=-1, keepdims=True)
    pe = jnp.exp2(s) if bounded else jnp.exp2(s - m)
    l = jnp.sum(pe, axis=-1, keepdims=True)
    return jnp.dot(pe.astype(BF16), v, preferred_element_type=F32), l, m


LOGIT_RANGE = 60.0
ATTN_A_LOOKAHEAD = 8


def _attn_a_body(bounded, q_ref, k_ref, v_ref, bias_ref, o_ref, nat, cm4, res2, res3):
    dil4, dil16 = DILATED_PATTERNS[1][1], DILATED_PATTERNS[2][1]
    cls_len = SEQ // dil4
    for i, ref in enumerate((q_ref, k_ref, v_ref)):
        nat[i] = ref[...].astype(F32)
        for c in range(dil4):
            cm4[i, c * cls_len:(c + 1) * cls_len, :] = nat[i, pl.ds(c, cls_len, stride=dil4), :]

    def operands(q_rows, kv_rows):
        return (cm4[0, q_rows, :].astype(BF16), cm4[1, kv_rows, :].astype(BF16), cm4[2, kv_rows, :].astype(BF16))

    def save(res, rows, acc, l, m):
        res[0, rows, :] = acc
        res[1, rows, :] = jnp.broadcast_to(l, acc.shape)
        if not bounded:
            res[2, rows, :] = jnp.broadcast_to(m, acc.shape)

    tasks = []

    def add_task(q_rows, kv_rows, bias, res, out_rows):
        def start():
            q, k, v = operands(q_rows, kv_rows)
            return _scores(q, k, bias()), v

        tasks.append((start, lambda s, v: save(res, out_rows, *_softmax_pv(s, v, bounded))))

    for c in range(dil4):
        for n in range(cls_len // BLK):
            lo = c * cls_len + n * BLK
            out_rows = pl.ds(n * BLK * dil4 + c, BLK, stride=dil4)
            if n == 0:
                add_task(slice(lo, lo + BLK), slice(lo, lo + BLK), lambda: bias_ref[1, :, BLK:], res2, out_rows)
            else:
                add_task(slice(lo, lo + BLK), slice(lo - BLK, lo + BLK), lambda: bias_ref[1], res2, out_rows)

    for c in range(dil16):
        rows = pl.ds((c % dil4) * cls_len + c // dil4, BLK, stride=dil16 // dil4)
        add_task(rows, rows, lambda: bias_ref[2, :, BLK:], res3, slice(c * P3_PITCH, c * P3_PITCH + BLK))

    def token_order(i, g):
        per_class = BLK // dil16
        return jnp.concatenate(
            [res3[i, pl.ds(g * per_class + j, dil16, stride=P3_PITCH), :] for j in range(per_class)], axis=0)

    def add_merge_task(g):
        rows = slice(g * BLK, (g + 1) * BLK)
        keys = rows if g == 0 else slice((g - 1) * BLK, (g + 1) * BLK)

        def start():
            bias = bias_ref[0, :, BLK:] if g == 0 else bias_ref[0]
            return _scores(q_ref[rows, :], k_ref[keys, :], bias), v_ref[keys, :]

        def finish(s, v):
            acc1, l1, m1 = _softmax_pv(s, v, bounded)
            if bounded:
                num = acc1 + res2[0, rows, :] + token_order(0, g)
                den = l1 + res2[1, rows, :] + token_order(1, g)
            else:
                m2, m3 = res2[2, rows, :], token_order(2, g)
                top = jnp.maximum(jnp.maximum(m1, m2), m3)
                w1, w2, w3 = jnp.exp2(m1 - top), jnp.exp2(m2 - top), jnp.exp2(m3 - top)
                num = w1 * acc1 + w2 * res2[0, rows, :] + w3 * token_order(0, g)
                den = w1 * l1 + w2 * res2[1, rows, :] + w3 * token_order(1, g)
            o_ref[rows, :] = (num * (1.0 / den)).astype(BF16)

        tasks.append((start, finish))

    for g in range(NBLK):
        add_merge_task(g)
    _emit_pipelined(tasks, ATTN_A_LOOKAHEAD)


def _attn_a(proj, bias, batch, bounded):
    n_pat = len(DILATED_PATTERNS)
    head = lambda col0: pl.BlockSpec((SEQ, HEAD_DIM), lambda b, h: (b, col0 + h))
    p3_rows = DILATED_PATTERNS[2][1] * P3_PITCH
    n_stats = 2 if bounded else 3
    return pl.pallas_call(
        functools.partial(_attn_a_body, bounded),
        grid=(batch, A_HEADS),
        in_specs=[head(COL_QA), head(COL_KA), head(COL_VA),
                  pl.BlockSpec((n_pat, None, BLK, 2 * BLK), lambda b, h: (0, h, 0, 0))],
        out_specs=pl.BlockSpec((SEQ, HEAD_DIM), lambda b, h: (b, h)),
        out_shape=jax.ShapeDtypeStruct((batch * SEQ, A_WIDTH), BF16),
        scratch_shapes=[pltpu.VMEM((3, SEQ, HEAD_DIM), F32)] * 2 + [
            pltpu.VMEM((n_stats, SEQ, HEAD_DIM), F32), pltpu.VMEM((n_stats, p3_rows, HEAD_DIM), F32)],
        compiler_params=pltpu.CompilerParams(
            dimension_semantics=("parallel", "parallel"), vmem_limit_bytes=VMEM_LIMIT),
        name="attn_dilated_bounded" if bounded else "attn_dilated",
    )(proj, proj, proj, bias)


FOX_TQ = 256


def _attn_b_body(bounded, q_ref, k_ref, v_ref, c_ref, ccol_ref, o_ref):
    h = pl.program_id(1)
    ck = c_ref[pl.ds(h, 1), :]
    qi = lax.broadcasted_iota(jnp.int32, (FOX_TQ, FOX_TQ), 0)
    kj = lax.broadcasted_iota(jnp.int32, (FOX_TQ, FOX_TQ), 1)
    causal = kj <= qi
    head_lane = lax.broadcasted_iota(jnp.int32, (FOX_TQ, LANES), 1) == h

    def task(i):
        lo, hi = i * FOX_TQ, (i + 1) * FOX_TQ

        def start():
            q = q_ref[lo:hi, :]
            s_diag = lax.dot_general(q, k_ref[lo:hi, :], _NT, preferred_element_type=F32)
            s_off = lax.dot_general(q, k_ref[:lo, :], _NT, preferred_element_type=F32) if i > 0 else None
            return s_diag, s_off

        def finish(s_diag, s_off):
            if bounded:
                cq = jnp.sum(jnp.where(head_lane, ccol_ref[lo:hi, :], 0.0), axis=-1, keepdims=True)
                p_diag = jnp.exp2(jnp.where(causal, (s_diag + cq) - ck[:, lo:hi], NEG))
                p_off = jnp.exp2((s_off + cq) - ck[:, :lo]) if i > 0 else None
            else:
                s_diag = jnp.where(causal, s_diag - ck[:, lo:hi], NEG)
                m = jnp.max(s_diag, axis=-1, keepdims=True)
                if i > 0:
                    s_off = s_off - ck[:, :lo]
                    m = jnp.maximum(m, jnp.max(s_off, axis=-1, keepdims=True))
                p_diag = jnp.exp2(s_diag - m)
                p_off = jnp.exp2(s_off - m) if i > 0 else None
            l = jnp.sum(p_diag, axis=-1, keepdims=True)
            acc = jnp.dot(p_diag.astype(BF16), v_ref[lo:hi, :], preferred_element_type=F32)
            if i > 0:
                l = l + jnp.sum(p_off, axis=-1, keepdims=True)
                acc = acc + jnp.dot(p_off.astype(BF16), v_ref[:lo, :], preferred_element_type=F32)
            o_ref[lo:hi, :] = (acc * (1.0 / l)).astype(BF16)

        return start, finish

    _emit_pipelined([task(i) for i in range(SEQ // FOX_TQ)], 2)


def _attn_b(proj, c, ccol, batch, bounded):
    head = lambda col0: pl.BlockSpec((SEQ, HEAD_DIM), lambda b, h: (b, col0 + h))
    return pl.pallas_call(
        functools.partial(_attn_b_body, bounded),
        grid=(batch, B_HEADS),
        in_specs=[head(COL_QB), head(COL_KB), head(COL_VB),
                  pl.BlockSpec((None, B_HEADS, SEQ), lambda b, h: (b, 0, 0)),
                  pl.BlockSpec((SEQ, LANES), lambda b, h: (b, 0))],
        out_specs=pl.BlockSpec((SEQ, HEAD_DIM), lambda b, h: (b, h)),
        out_shape=jax.ShapeDtypeStruct((batch * SEQ, B_WIDTH), BF16),
        compiler_params=pltpu.CompilerParams(
            dimension_semantics=("parallel", "parallel"), vmem_limit_bytes=VMEM_LIMIT),
        name="attn_forgetting_bounded" if bounded else "attn_forgetting",
    )(proj, proj, proj, c, ccol)


def _mem_kv_body(mem_ref, g_ref, w_ref, gain_ref, out_ref):
    hb = (_rms_scale(mem_ref[...]) * g_ref[...]).astype(BF16)
    acc = jnp.dot(hb, w_ref[...], preferred_element_type=F32)
    gain = gain_ref[...]
    for s in range(M_HEADS):
        sl = slice(s * M_HEAD_DIM, (s + 1) * M_HEAD_DIM)
        out_ref[:, sl] = (_rms_scale(acc[:, sl]) * gain[:, sl]).astype(BF16)
    out_ref[:, M_WIDTH:] = acc[:, M_WIDTH:].astype(BF16)


def _mem_kv(mem2, g, w, gain, batch):
    return pl.pallas_call(
        _mem_kv_body,
        grid=(batch,),
        in_specs=[
            pl.BlockSpec((MEM_LEN, D_MODEL), lambda b: (b, 0)),
            pl.BlockSpec((1, D_MODEL), lambda b: (0, 0)),
            pl.BlockSpec((D_MODEL, 2 * M_WIDTH), lambda b: (0, 0)),
            pl.BlockSpec((1, M_WIDTH), lambda b: (0, 0)),
        ],
        out_specs=pl.BlockSpec((MEM_LEN, 2 * M_WIDTH), lambda b: (b, 0)),
        out_shape=jax.ShapeDtypeStruct((batch * MEM_LEN, 2 * M_WIDTH), BF16),
        compiler_params=pltpu.CompilerParams(
            dimension_semantics=("parallel",), vmem_limit_bytes=VMEM_LIMIT),
        name="mem_kv",
    )(mem2, g, w, gain)


MEM_TQ = 512


def _attn_m_body(bounded, q_ref, k_ref, v_ref, o_ref):
    def task(i):
        rows = slice(i * MEM_TQ, (i + 1) * MEM_TQ)

        def finish(s):
            acc, l, _ = _softmax_pv(s, v_ref[...], bounded)
            o_ref[rows, :] = (acc * (1.0 / l)).astype(BF16)

        return lambda: (lax.dot_general(q_ref[rows, :], k_ref[...], _NT, preferred_element_type=F32),), finish

    _emit_pipelined([task(i) for i in range(SEQ // MEM_TQ)], 1)


def _attn_m(proj, kv, batch, bounded):
    col_qm = COL_QM * LANES // M_HEAD_DIM
    return pl.pallas_call(
        functools.partial(_attn_m_body, bounded),
        grid=(batch, M_HEADS),
        in_specs=[
            pl.BlockSpec((SEQ, M_HEAD_DIM), lambda b, h: (b, col_qm + h)),
            pl.BlockSpec((MEM_LEN, M_HEAD_DIM), lambda b, h: (b, h)),
            pl.BlockSpec((MEM_LEN, M_HEAD_DIM), lambda b, h: (b, M_HEADS + h)),
        ],
        out_specs=pl.BlockSpec((SEQ, M_HEAD_DIM), lambda b, h: (b, h)),
        out_shape=jax.ShapeDtypeStruct((batch * SEQ, M_WIDTH), BF16),
        compiler_params=pltpu.CompilerParams(
            dimension_semantics=("parallel", "parallel"), vmem_limit_bytes=VMEM_LIMIT),
        name="attn_memory_bounded" if bounded else "attn_memory",
    )(proj, kv, kv)


MERGE_TM = 256


def _merge_body(x_ref, oa_ref, za_ref, ob_ref, zb_ref, om_ref, zm_ref, gl_ref, bg_ref,
                wa_ref, wb_ref, wm_ref, wo_ref, out_ref):
    def branch(o_ref, z_ref, w_ref):
        z = z_ref[...].astype(F32)
        gated = o_ref[...].astype(F32) * (z * jax.nn.sigmoid(z))
        return jnp.dot(gated.astype(BF16), w_ref[...], preferred_element_type=F32)

    merged = None
    for n, (o_ref, z_ref, w_ref) in enumerate(((oa_ref, za_ref, wa_ref), (ob_ref, zb_ref, wb_ref),
                                                (om_ref, zm_ref, wm_ref))):
        cols = slice(n * D_MODEL, (n + 1) * D_MODEL)
        gate = jax.nn.sigmoid(gl_ref[:, cols].astype(F32) + bg_ref[:, cols])
        term = gate * branch(o_ref, z_ref, w_ref)
        merged = term if merged is None else merged + term
    out_ref[...] = x_ref[...] + jnp.dot(merged.astype(BF16), wo_ref[...], preferred_element_type=F32)


def _merge(x2, oa, ob, om, proj, bg, wa, wb, wm, wo):
    rows = x2.shape[0]
    row_blk = lambda width, colblk: pl.BlockSpec((MERGE_TM, width), lambda i: (i, colblk))
    resident = lambda shape: pl.BlockSpec(shape, lambda i: (0, 0), pipeline_mode=pl.Buffered(1))
    assert (COL_ZA * LANES) % A_WIDTH == 0 and (COL_ZB * LANES) % B_WIDTH == 0
    assert (COL_ZM * LANES) % M_WIDTH == 0 and (COL_GL * LANES) % (N_BRANCH * D_MODEL) == 0
    return pl.pallas_call(
        _merge_body,
        grid=(rows // MERGE_TM,),
        in_specs=[
            row_blk(D_MODEL, 0),
            row_blk(A_WIDTH, 0), row_blk(A_WIDTH, COL_ZA * LANES // A_WIDTH),
            row_blk(B_WIDTH, 0), row_blk(B_WIDTH, COL_ZB * LANES // B_WIDTH),
            row_blk(M_WIDTH, 0), row_blk(M_WIDTH, COL_ZM * LANES // M_WIDTH),
            row_blk(N_BRANCH * D_MODEL, COL_GL * LANES // (N_BRANCH * D_MODEL)),
            resident((1, N_BRANCH * D_MODEL)),
            resident((A_WIDTH, D_MODEL)), resident((B_WIDTH, D_MODEL)), resident((M_WIDTH, D_MODEL)),
            resident((D_MODEL, D_MODEL)),
        ],
        out_specs=row_blk(D_MODEL, 0),
        out_shape=jax.ShapeDtypeStruct((rows, D_MODEL), F32),
        compiler_params=pltpu.CompilerParams(
            dimension_semantics=("parallel",), vmem_limit_bytes=VMEM_LIMIT),
        name="merge_out",
    )(x2, oa, proj, ob, proj, om, proj, proj, bg, wa, wb, wm, wo)


def _layer(x2, mem2, batch, norm_g, mem_norm_g, w_in, b_forget, b_gate, rel_bias, q_norm_a, k_norm_a,
           q_norm_b, k_norm_b, q_norm_m, k_norm_m, w_mem_kv, w_proj_a, w_proj_b, w_proj_m, w_out):
    f_col = COL_QM * LANES
    w_f = jnp.pad(w_in[:, f_col:f_col + B_HEADS], ((0, 0), (0, LANES - B_HEADS))).astype(BF16)
    b_f = jnp.pad(b_forget.astype(F32), (0, LANES - B_HEADS))[None]
    scale = LOG2E / math.sqrt(HEAD_DIM)
    scale_m = LOG2E / math.sqrt(M_HEAD_DIM)
    ones = lambda n: jnp.ones((n,), F32)
    gain = jnp.concatenate([
        jnp.tile(q_norm_a.astype(F32) * scale, A_HEADS), jnp.tile(k_norm_a.astype(F32), A_HEADS), ones(2 * A_WIDTH),
        jnp.tile(q_norm_b.astype(F32) * scale, B_HEADS), jnp.tile(k_norm_b.astype(F32), B_HEADS), ones(2 * B_WIDTH),
        jnp.tile(q_norm_m.astype(F32) * scale_m, M_HEADS), ones(PROJ_COLS - COL_ZM * LANES)])[None]

    h, logf = _norm(x2, norm_g.astype(F32)[None], w_f, b_f)
    proj = _proj(h, w_in.astype(F32), gain)

    def logit_bound(gq, gk, dim, scl):
        return dim * scl * jnp.max(jnp.abs(gq.astype(F32))) * jnp.max(jnp.abs(gk.astype(F32)))

    def dispatch(bound, call, *operands):
        return lax.cond(bound <= LOGIT_RANGE, functools.partial(call, bounded=True),
                        functools.partial(call, bounded=False), *operands)

    bound_a = logit_bound(q_norm_a, k_norm_a, HEAD_DIM, scale) + LOG2E * jnp.max(jnp.abs(rel_bias.astype(F32)))
    oa = dispatch(bound_a, lambda p, b, bounded: _attn_a(p, b, batch, bounded),
                  proj, _bias_tiles(rel_bias.astype(F32)))

    c, ccol = _cumsum(logf, batch)
    ob = dispatch(logit_bound(q_norm_b, k_norm_b, HEAD_DIM, scale),
                  lambda p, c_, cc, bounded: _attn_b(p, c_, cc, batch, bounded), proj, c, ccol)

    kv = _mem_kv(mem2, mem_norm_g.astype(F32)[None], w_mem_kv.astype(BF16),
                 jnp.tile(k_norm_m.astype(F32), M_HEADS)[None], batch)
    om = dispatch(logit_bound(q_norm_m, k_norm_m, M_HEAD_DIM, scale_m),
                  lambda p, kv_, bounded: _attn_m(p, kv_, batch, bounded), proj, kv)

    return _merge(x2, oa, ob, om, proj, b_gate.astype(F32).reshape(1, N_BRANCH * D_MODEL),
                  w_proj_a.astype(BF16), w_proj_b.astype(BF16), w_proj_m.astype(BF16), w_out.astype(BF16))


def kernel(x, mem, norm_g, mem_norm_g, w_in, b_forget, b_gate, rel_bias, q_norm_a, k_norm_a, q_norm_b, k_norm_b,
           q_norm_m, k_norm_m, w_mem_kv, w_proj_a, w_proj_b, w_proj_m, w_out):
    batch, seq, d_model = x.shape
    assert (seq, d_model) == (SEQ, D_MODEL) and mem.shape == (batch, MEM_LEN, D_MODEL)
    x2 = x.reshape(batch * seq, d_model)
    mem2 = mem.reshape(batch * MEM_LEN, d_model)
    depth = w_in.shape[0]
    for l in range(depth):
        x2 = _layer(x2, mem2, batch, norm_g[l], mem_norm_g[l], w_in[l], b_forget[l], b_gate[l], rel_bias,
                    q_norm_a[l], k_norm_a[l], q_norm_b[l], k_norm_b[l], q_norm_m[l], k_norm_m[l],
                    w_mem_kv[l], w_proj_a[l], w_proj_b[l], w_proj_m[l], w_out[l])
    return x2.reshape(batch, seq, d_model)
```

```python
import functools
import math

import numpy as np
import jax
import jax.numpy as jnp
from jax import lax
from jax.experimental import pallas as pl
from jax.experimental.pallas import tpu as pltpu

F32 = jnp.float32
BF16 = jnp.bfloat16

D_MODEL = 2048
SEQ = 2048
HEAD_DIM = 128
A_HEADS = 12
B_HEADS = 8
M_HEADS = 4
M_HEAD_DIM = 256
MEM_LEN = 256
A_WIDTH = A_HEADS * HEAD_DIM
B_WIDTH = B_HEADS * HEAD_DIM
M_WIDTH = M_HEADS * M_HEAD_DIM
N_BRANCH = 3
BLK = 128
NBLK = SEQ // BLK
DILATED_PATTERNS = ((128, 1), (512, 4), (2048, 16))
REL_BUCKETS = 32
REL_MAX_DIST = 2048
EPS = 1e-6
NEG = -1e30
LOG2E = math.log2(math.e)

LANES = 128
VMEM_LIMIT = 56 * 1024 * 1024

COL_QA = 0
COL_KA = COL_QA + A_HEADS
COL_VA = COL_KA + A_HEADS
COL_ZA = COL_VA + A_HEADS
COL_QB = COL_ZA + A_HEADS
COL_KB = COL_QB + B_HEADS
COL_VB = COL_KB + B_HEADS
COL_ZB = COL_VB + B_HEADS
COL_QM = COL_ZB + B_HEADS
COL_ZM = COL_QM + M_WIDTH // LANES
COL_GL = COL_ZM + M_WIDTH // LANES
COL_END = COL_GL + N_BRANCH * D_MODEL // LANES
PROJ_COLS = COL_END * LANES

NORM_TM = 1024
PROJ_TM = 2048
PROJ_TN = 1024
PROJ_CHUNK = 128
PROJ_NB_HEAD = COL_QM * LANES // PROJ_TN
F_SHIFT = B_HEADS


def _log_sigmoid(x):
    return jnp.minimum(x, 0.0) - jnp.log1p(jnp.exp(-jnp.abs(x)))


def _rms_scale(a):
    return a * lax.rsqrt(jnp.mean(a * a, axis=-1, keepdims=True) + EPS)


def _emit_pipelined(tasks, lookahead):
    pending = []
    for start, finish in tasks:
        pending.append((finish, start()))
        if len(pending) > lookahead:
            fin, state = pending.pop(0)
            fin(*state)
    for fin, state in pending:
        fin(*state)


def _col_block_range(col0, width):
    return col0 * LANES // PROJ_TN, (col0 * LANES + width) // PROJ_TN


def _in_col_blocks(j, *ranges):
    hit = None
    for lo, hi in ranges:
        cond = (j >= lo) & (j < hi)
        hit = cond if hit is None else hit | cond
    return hit


def _norm_body(x_ref, g_ref, wf_ref, bf_ref, h_ref, f_ref):
    hb = (_rms_scale(x_ref[...]) * g_ref[...]).astype(BF16)
    h_ref[...] = hb
    f = jnp.dot(hb, wf_ref[...], preferred_element_type=F32) + bf_ref[...]
    f_ref[...] = _log_sigmoid(f)


def _norm(x2, g, wf, bfp):
    rows = x2.shape[0]
    return pl.pallas_call(
        _norm_body,
        grid=(rows // NORM_TM,),
        in_specs=[
            pl.BlockSpec((NORM_TM, D_MODEL), lambda i: (i, 0)),
            pl.BlockSpec((1, D_MODEL), lambda i: (0, 0)),
            pl.BlockSpec((D_MODEL, LANES), lambda i: (0, 0)),
            pl.BlockSpec((1, LANES), lambda i: (0, 0)),
        ],
        out_specs=[pl.BlockSpec((NORM_TM, D_MODEL), lambda i: (i, 0)),
                   pl.BlockSpec((NORM_TM, LANES), lambda i: (i, 0))],
        out_shape=[jax.ShapeDtypeStruct((rows, D_MODEL), BF16), jax.ShapeDtypeStruct((rows, LANES), F32)],
        compiler_params=pltpu.CompilerParams(dimension_semantics=("parallel",), vmem_limit_bytes=VMEM_LIMIT),
        name="norm",
    )(x2, g, wf, bfp)


def _proj_body(h_ref, wa_ref, wb_ref, gain_ref, out_ref, w16_ref):
    j = pl.program_id(0)

    @pl.when((pl.program_id(1) == 0) & (j < PROJ_NB_HEAD))
    def _():
        w16_ref[...] = wa_ref[...].astype(BF16)

    @pl.when((pl.program_id(1) == 0) & (j >= PROJ_NB_HEAD))
    def _():
        wide = jnp.concatenate([wa_ref[...], wb_ref[...]], axis=1)
        w16_ref[...] = pltpu.roll(wide, wide.shape[1] - F_SHIFT, axis=1)[:, :PROJ_TN].astype(BF16)

    def plain(acc, rows):
        out_ref[rows, :] = acc.astype(BF16)

    def head_normed(hd):
        def epilogue(acc, rows):
            gain = gain_ref[...]
            for s in range(PROJ_TN // hd):
                sl = slice(s * hd, (s + 1) * hd)
                out_ref[rows, sl] = (_rms_scale(acc[:, sl]) * gain[:, sl]).astype(BF16)
        return epilogue

    def run(epilogue):
        tasks = []
        for r in range(PROJ_TM // PROJ_CHUNK):
            rows = slice(r * PROJ_CHUNK, (r + 1) * PROJ_CHUNK)
            tasks.append((lambda rows=rows: (jnp.dot(h_ref[rows, :], w16_ref[...], preferred_element_type=F32),),
                          lambda acc, rows=rows: epilogue(acc, rows)))
        _emit_pipelined(tasks, 1)

    head128 = _in_col_blocks(j, _col_block_range(COL_QA, 2 * A_WIDTH), _col_block_range(COL_QB, 2 * B_WIDTH))
    head256 = _in_col_blocks(j, _col_block_range(COL_QM, M_WIDTH))
    pl.when(head128)(lambda: run(head_normed(HEAD_DIM)))
    pl.when(head256)(lambda: run(head_normed(M_HEAD_DIM)))
    pl.when(jnp.logical_not(head128 | head256))(lambda: run(plain))


def _proj(h, w_in, layer, gain):
    rows = h.shape[0]
    lanes_per_block = PROJ_TN // LANES

    def w_main(j, i):
        return layer, 0, j

    def w_spill(j, i):
        return layer, 0, (jnp.maximum(j, PROJ_NB_HEAD) + 1) * lanes_per_block

    return pl.pallas_call(
        _proj_body,
        grid=(PROJ_COLS // PROJ_TN, rows // PROJ_TM),
        in_specs=[
            pl.BlockSpec((PROJ_TM, D_MODEL), lambda j, i: (i, 0)),
            pl.BlockSpec((None, D_MODEL, PROJ_TN), w_main),
            pl.BlockSpec((None, D_MODEL, LANES), w_spill),
            pl.BlockSpec((1, PROJ_TN), lambda j, i: (0, j)),
        ],
        out_specs=pl.BlockSpec((PROJ_TM, PROJ_TN), lambda j, i: (i, j)),
        out_shape=jax.ShapeDtypeStruct((rows, PROJ_COLS), BF16),
        scratch_shapes=[pltpu.VMEM((D_MODEL, PROJ_TN), BF16)],
        compiler_params=pltpu.CompilerParams(
            dimension_semantics=("parallel", "arbitrary"), vmem_limit_bytes=VMEM_LIMIT),
        name="proj",
    )(h, w_in, w_in, gain)


def _rel_bucket_np(dist):
    max_exact = REL_BUCKETS // 2
    d_f = np.maximum(dist, 1).astype(np.float32)
    large = max_exact + (np.log(d_f / np.float32(max_exact)) / np.float32(math.log(REL_MAX_DIST / max_exact))
                         * np.float32(REL_BUCKETS - max_exact)).astype(np.int32)
    large = np.minimum(large, REL_BUCKETS - 1)
    return np.where(dist < max_exact, dist, large).astype(np.int32)


def _bias_buckets():
    qi = np.arange(BLK)[:, None]
    kj = np.arange(2 * BLK)[None, :]
    delta = qi - kj + BLK
    tiles = []
    for window, dil in DILATED_PATTERNS:
        w_sub = window // dil
        valid = (delta >= 0) & (delta <= w_sub)
        tiles.append(np.where(valid, _rel_bucket_np(np.clip(delta, 0, w_sub) * dil), -1))
    return np.stack(tiles).astype(np.int32)


def _bias_body(tbl_ref, bucket_ref, out_ref):
    bucket = bucket_ref[...]
    for h in range(A_HEADS):
        out = jnp.full(bucket.shape, NEG, F32)
        for r in range(REL_BUCKETS):
            out = jnp.where(bucket == r, tbl_ref[r, h] * LOG2E, out)
        out_ref[h] = out


def _bias_tiles(rel_bias):
    n_pat = len(DILATED_PATTERNS)
    return pl.pallas_call(
        _bias_body,
        grid=(n_pat,),
        in_specs=[pl.BlockSpec(memory_space=pltpu.SMEM),
                  pl.BlockSpec((None, BLK, 2 * BLK), lambda p: (p, 0, 0))],
        out_specs=pl.BlockSpec((None, A_HEADS, BLK, 2 * BLK), lambda p: (p, 0, 0, 0)),
        out_shape=jax.ShapeDtypeStruct((n_pat, A_HEADS, BLK, 2 * BLK), F32),
        name="rel_bias_tiles",
    )(rel_bias, jnp.asarray(_bias_buckets()))


def _cumsum_body(f_ref, c_ref, ccol_ref):
    ft = f_ref[...].T[:B_HEADS, :]
    pos = lax.broadcasted_iota(jnp.int32, ft.shape, 1)
    shift = 1
    while shift < SEQ:
        ft = ft + jnp.where(pos >= shift, pltpu.roll(ft, shift, axis=1), 0.0)
        shift *= 2
    c = ft * LOG2E
    c_ref[...] = c
    ccol_ref[...] = jnp.concatenate([c, jnp.zeros((LANES - B_HEADS, SEQ), F32)], axis=0).T


def _cumsum(logf, batch):
    return pl.pallas_call(
        _cumsum_body,
        grid=(batch,),
        in_specs=[pl.BlockSpec((SEQ, LANES), lambda b: (b, 0))],
        out_specs=[pl.BlockSpec((None, B_HEADS, SEQ), lambda b: (b, 0, 0)),
                   pl.BlockSpec((SEQ, LANES), lambda b: (b, 0))],
        out_shape=[jax.ShapeDtypeStruct((batch, B_HEADS, SEQ), F32),
                   jax.ShapeDtypeStruct((batch * SEQ, LANES), F32)],
        name="forget_cumsum",
    )(logf)


_NT = (((1,), (1,)), ((), ()))


P3_PITCH = BLK + 8


def _scores(q, k, bias):
    return lax.dot_general(q, k, _NT, preferred_element_type=F32) + bias


def _softmax_pv(s, v, bounded):
    m = 0.0 if bounded else jnp.max(s, axis=-1, keepdims=True)
    pe = jnp.exp2(s) if bounded else jnp.exp2(s - m)
    l = jnp.sum(pe, axis=-1, keepdims=True)
    return jnp.dot(pe.astype(BF16), v, preferred_element_type=F32), l, m


LOGIT_RANGE = 60.0
ATTN_A_LOOKAHEAD = 8


def _attn_a_body(bounded, q_ref, k_ref, v_ref, bias_ref, o_ref, nat, cm4, res2, res3):
    dil4, dil16 = DILATED_PATTERNS[1][1], DILATED_PATTERNS[2][1]
    cls_len = SEQ // dil4
    for i, ref in enumerate((q_ref, k_ref, v_ref)):
        nat[i] = ref[...].astype(F32)
        for c in range(dil4):
            cm4[i, c * cls_len:(c + 1) * cls_len, :] = nat[i, pl.ds(c, cls_len, stride=dil4), :]

    def operands(q_rows, kv_rows):
        return (cm4[0, q_rows, :].astype(BF16), cm4[1, kv_rows, :].astype(BF16), cm4[2, kv_rows, :].astype(BF16))

    def save(res, rows, acc, l, m):
        res[0, rows, :] = acc
        res[1, rows, :] = jnp.broadcast_to(l, acc.shape)
        if not bounded:
            res[2, rows, :] = jnp.broadcast_to(m, acc.shape)

    tasks = []

    def add_task(q_rows, kv_rows, bias, res, out_rows):
        def start():
            q, k, v = operands(q_rows, kv_rows)
            return _scores(q, k, bias()), v

        tasks.append((start, lambda s, v: save(res, out_rows, *_softmax_pv(s, v, bounded))))

    for c in range(dil4):
        for n in range(cls_len // BLK):
            lo = c * cls_len + n * BLK
            out_rows = pl.ds(n * BLK * dil4 + c, BLK, stride=dil4)
            if n == 0:
                add_task(slice(lo, lo + BLK), slice(lo, lo + BLK), lambda: bias_ref[1, :, BLK:], res2, out_rows)
            else:
                add_task(slice(lo, lo + BLK), slice(lo - BLK, lo + BLK), lambda: bias_ref[1], res2, out_rows)

    for c in range(dil16):
        rows = pl.ds((c % dil4) * cls_len + c // dil4, BLK, stride=dil16 // dil4)
        add_task(rows, rows, lambda: bias_ref[2, :, BLK:], res3, slice(c * P3_PITCH, c * P3_PITCH + BLK))

    def token_order(i, g):
        per_class = BLK // dil16
        return jnp.concatenate(
            [res3[i, pl.ds(g * per_class + j, dil16, stride=P3_PITCH), :] for j in range(per_class)], axis=0)

    def add_merge_task(g):
        rows = slice(g * BLK, (g + 1) * BLK)
        keys = rows if g == 0 else slice((g - 1) * BLK, (g + 1) * BLK)

        def start():
            bias = bias_ref[0, :, BLK:] if g == 0 else bias_ref[0]
            return _scores(q_ref[rows, :], k_ref[keys, :], bias), v_ref[keys, :]

        def finish(s, v):
            acc1, l1, m1 = _softmax_pv(s, v, bounded)
            if bounded:
                num = acc1 + res2[0, rows, :] + token_order(0, g)
                den = l1 + res2[1, rows, :] + token_order(1, g)
            else:
                m2, m3 = res2[2, rows, :], token_order(2, g)
                top = jnp.maximum(jnp.maximum(m1, m2), m3)
                w1, w2, w3 = jnp.exp2(m1 - top), jnp.exp2(m2 - top), jnp.exp2(m3 - top)
                num = w1 * acc1 + w2 * res2[0, rows, :] + w3 * token_order(0, g)
                den = w1 * l1 + w2 * res2[1, rows, :] + w3 * token_order(1, g)
            o_ref[rows, :] = (num * (1.0 / den)).astype(BF16)

        tasks.append((start, finish))

    for g in range(NBLK):
        add_merge_task(g)
    _emit_pipelined(tasks, ATTN_A_LOOKAHEAD)


def _attn_a(proj, bias, batch, bounded):
    n_pat = len(DILATED_PATTERNS)
    head = lambda col0: pl.BlockSpec((SEQ, HEAD_DIM), lambda b, h: (b, col0 + h))
    p3_rows = DILATED_PATTERNS[2][1] * P3_PITCH
    n_stats = 2 if bounded else 3
    return pl.pallas_call(
        functools.partial(_attn_a_body, bounded),
        grid=(batch, A_HEADS),
        in_specs=[head(COL_QA), head(COL_KA), head(COL_VA),
                  pl.BlockSpec((n_pat, None, BLK, 2 * BLK), lambda b, h: (0, h, 0, 0))],
        out_specs=pl.BlockSpec((SEQ, HEAD_DIM), lambda b, h: (b, h)),
        out_shape=jax.ShapeDtypeStruct((batch * SEQ, A_WIDTH), BF16),
        scratch_shapes=[pltpu.VMEM((3, SEQ, HEAD_DIM), F32)] * 2 + [
            pltpu.VMEM((n_stats, SEQ, HEAD_DIM), F32), pltpu.VMEM((n_stats, p3_rows, HEAD_DIM), F32)],
        compiler_params=pltpu.CompilerParams(
            dimension_semantics=("parallel", "parallel"), vmem_limit_bytes=VMEM_LIMIT),
        name="attn_dilated_bounded" if bounded else "attn_dilated",
    )(proj, proj, proj, bias)


FOX_TQ = 256


def _attn_b_body(bounded, q_ref, k_ref, v_ref, c_ref, ccol_ref, o_ref):
    h = pl.program_id(1)
    ck = c_ref[pl.ds(h, 1), :]
    qi = lax.broadcasted_iota(jnp.int32, (FOX_TQ, FOX_TQ), 0)
    kj = lax.broadcasted_iota(jnp.int32, (FOX_TQ, FOX_TQ), 1)
    causal = kj <= qi
    head_lane = lax.broadcasted_iota(jnp.int32, (FOX_TQ, LANES), 1) == h

    def task(i):
        lo, hi = i * FOX_TQ, (i + 1) * FOX_TQ

        def start():
            q = q_ref[lo:hi, :]
            s_diag = lax.dot_general(q, k_ref[lo:hi, :], _NT, preferred_element_type=F32)
            s_off = lax.dot_general(q, k_ref[:lo, :], _NT, preferred_element_type=F32) if i > 0 else None
            return s_diag, s_off

        def finish(s_diag, s_off):
            if bounded:
                cq = jnp.sum(jnp.where(head_lane, ccol_ref[lo:hi, :], 0.0), axis=-1, keepdims=True)
                p_diag = jnp.exp2(jnp.where(causal, (s_diag + cq) - ck[:, lo:hi], NEG))
                p_off = jnp.exp2((s_off + cq) - ck[:, :lo]) if i > 0 else None
            else:
                s_diag = jnp.where(causal, s_diag - ck[:, lo:hi], NEG)
                m = jnp.max(s_diag, axis=-1, keepdims=True)
                if i > 0:
                    s_off = s_off - ck[:, :lo]
                    m = jnp.maximum(m, jnp.max(s_off, axis=-1, keepdims=True))
                p_diag = jnp.exp2(s_diag - m)
                p_off = jnp.exp2(s_off - m) if i > 0 else None
            l = jnp.sum(p_diag, axis=-1, keepdims=True)
            acc = jnp.dot(p_diag.astype(BF16), v_ref[lo:hi, :], preferred_element_type=F32)
            if i > 0:
                l = l + jnp.sum(p_off, axis=-1, keepdims=True)
                acc = acc + jnp.dot(p_off.astype(BF16), v_ref[:lo, :], preferred_element_type=F32)
            o_ref[lo:hi, :] = (acc * (1.0 / l)).astype(BF16)

        return start, finish

    _emit_pipelined([task(i) for i in range(SEQ // FOX_TQ)], 2)


def _attn_b(proj, c, ccol, batch, bounded):
    head = lambda col0: pl.BlockSpec((SEQ, HEAD_DIM), lambda b, h: (b, col0 + h))
    return pl.pallas_call(
        functools.partial(_attn_b_body, bounded),
        grid=(batch, B_HEADS),
        in_specs=[head(COL_QB), head(COL_KB), head(COL_VB),
                  pl.BlockSpec((None, B_HEADS, SEQ), lambda b, h: (b, 0, 0)),
                  pl.BlockSpec((SEQ, LANES), lambda b, h: (b, 0))],
        out_specs=pl.BlockSpec((SEQ, HEAD_DIM), lambda b, h: (b, h)),
        out_shape=jax.ShapeDtypeStruct((batch * SEQ, B_WIDTH), BF16),
        compiler_params=pltpu.CompilerParams(
            dimension_semantics=("parallel", "parallel"), vmem_limit_bytes=VMEM_LIMIT),
        name="attn_forgetting_bounded" if bounded else "attn_forgetting",
    )(proj, proj, proj, c, ccol)


def _mem_kv_body(mem_ref, g_ref, w_ref, gain_ref, out_ref):
    hb = (_rms_scale(mem_ref[...]) * g_ref[...]).astype(BF16)
    acc = jnp.dot(hb, w_ref[...], preferred_element_type=F32)
    gain = gain_ref[...]
    for s in range(M_HEADS):
        sl = slice(s * M_HEAD_DIM, (s + 1) * M_HEAD_DIM)
        out_ref[:, sl] = (_rms_scale(acc[:, sl]) * gain[:, sl]).astype(BF16)
    out_ref[:, M_WIDTH:] = acc[:, M_WIDTH:].astype(BF16)


def _mem_kv(mem2, g, w, gain, batch):
    return pl.pallas_call(
        _mem_kv_body,
        grid=(batch,),
        in_specs=[
            pl.BlockSpec((MEM_LEN, D_MODEL), lambda b: (b, 0)),
            pl.BlockSpec((1, D_MODEL), lambda b: (0, 0)),
            pl.BlockSpec((D_MODEL, 2 * M_WIDTH), lambda b: (0, 0)),
            pl.BlockSpec((1, M_WIDTH), lambda b: (0, 0)),
        ],
        out_specs=pl.BlockSpec((MEM_LEN, 2 * M_WIDTH), lambda b: (b, 0)),
        out_shape=jax.ShapeDtypeStruct((batch * MEM_LEN, 2 * M_WIDTH), BF16),
        compiler_params=pltpu.CompilerParams(
            dimension_semantics=("parallel",), vmem_limit_bytes=VMEM_LIMIT),
        name="mem_kv",
    )(mem2, g, w, gain)


MEM_TQ = 512


def _attn_m_body(bounded, q_ref, k_ref, v_ref, o_ref):
    def task(i):
        rows = slice(i * MEM_TQ, (i + 1) * MEM_TQ)

        def finish(s):
            acc, l, _ = _softmax_pv(s, v_ref[...], bounded)
            o_ref[rows, :] = (acc * (1.0 / l)).astype(BF16)

        return lambda: (lax.dot_general(q_ref[rows, :], k_ref[...], _NT, preferred_element_type=F32),), finish

    _emit_pipelined([task(i) for i in range(SEQ // MEM_TQ)], 1)


def _attn_m(proj, kv, batch, bounded):
    col_qm = COL_QM * LANES // M_HEAD_DIM
    return pl.pallas_call(
        functools.partial(_attn_m_body, bounded),
        grid=(batch, M_HEADS),
        in_specs=[
            pl.BlockSpec((SEQ, M_HEAD_DIM), lambda b, h: (b, col_qm + h)),
            pl.BlockSpec((MEM_LEN, M_HEAD_DIM), lambda b, h: (b, h)),
            pl.BlockSpec((MEM_LEN, M_HEAD_DIM), lambda b, h: (b, M_HEADS + h)),
        ],
        out_specs=pl.BlockSpec((SEQ, M_HEAD_DIM), lambda b, h: (b, h)),
        out_shape=jax.ShapeDtypeStruct((batch * SEQ, M_WIDTH), BF16),
        compiler_params=pltpu.CompilerParams(
            dimension_semantics=("parallel", "parallel"), vmem_limit_bytes=VMEM_LIMIT),
        name="attn_memory_bounded" if bounded else "attn_memory",
    )(proj, kv, kv)


MERGE_TM = 256


def _merge_body(x_ref, oa_ref, za_ref, ob_ref, zb_ref, om_ref, zm_ref, gl_ref, bg_ref,
                wa_ref, wb_ref, wm_ref, wo_ref, out_ref):
    def branch(o_ref, z_ref, w_ref):
        z = z_ref[...].astype(F32)
        gated = o_ref[...].astype(F32) * (z * jax.nn.sigmoid(z))
        return jnp.dot(gated.astype(BF16), w_ref[...], preferred_element_type=F32)

    merged = None
    for n, (o_ref, z_ref, w_ref) in enumerate(((oa_ref, za_ref, wa_ref), (ob_ref, zb_ref, wb_ref),
                                                (om_ref, zm_ref, wm_ref))):
        cols = slice(n * D_MODEL, (n + 1) * D_MODEL)
        gate = jax.nn.sigmoid(gl_ref[:, cols].astype(F32) + bg_ref[:, cols])
        term = gate * branch(o_ref, z_ref, w_ref)
        merged = term if merged is None else merged + term
    out_ref[...] = x_ref[...] + jnp.dot(merged.astype(BF16), wo_ref[...], preferred_element_type=F32)


def _merge(x2, oa, ob, om, proj, bg, wa, wb, wm, wo):
    rows = x2.shape[0]
    row_blk = lambda width, colblk: pl.BlockSpec((MERGE_TM, width), lambda i: (i, colblk))
    resident = lambda shape: pl.BlockSpec(shape, lambda i: (0, 0), pipeline_mode=pl.Buffered(1))
    assert (COL_ZA * LANES) % A_WIDTH == 0 and (COL_ZB * LANES) % B_WIDTH == 0
    assert (COL_ZM * LANES) % M_WIDTH == 0 and (COL_GL * LANES) % (N_BRANCH * D_MODEL) == 0
    return pl.pallas_call(
        _merge_body,
        grid=(rows // MERGE_TM,),
        in_specs=[
            row_blk(D_MODEL, 0),
            row_blk(A_WIDTH, 0), row_blk(A_WIDTH, COL_ZA * LANES // A_WIDTH),
            row_blk(B_WIDTH, 0), row_blk(B_WIDTH, COL_ZB * LANES // B_WIDTH),
            row_blk(M_WIDTH, 0), row_blk(M_WIDTH, COL_ZM * LANES // M_WIDTH),
            row_blk(N_BRANCH * D_MODEL, COL_GL * LANES // (N_BRANCH * D_MODEL)),
            resident((1, N_BRANCH * D_MODEL)),
            resident((A_WIDTH, D_MODEL)), resident((B_WIDTH, D_MODEL)), resident((M_WIDTH, D_MODEL)),
            resident((D_MODEL, D_MODEL)),
        ],
        out_specs=row_blk(D_MODEL, 0),
        out_shape=jax.ShapeDtypeStruct((rows, D_MODEL), F32),
        compiler_params=pltpu.CompilerParams(
            dimension_semantics=("parallel",), vmem_limit_bytes=VMEM_LIMIT),
        name="merge_out",
    )(x2, oa, proj, ob, proj, om, proj, proj, bg, wa, wb, wm, wo)


def _layer(x2, mem2, batch, layer, norm_g, mem_norm_g, w_in, b_forget, b_gate, rel_bias, q_norm_a, k_norm_a,
           q_norm_b, k_norm_b, q_norm_m, k_norm_m, w_mem_kv, w_proj_a, w_proj_b, w_proj_m, w_out):
    f_col = COL_QM * LANES
    w_f = jnp.pad(w_in[layer, :, f_col:f_col + B_HEADS], ((0, 0), (0, LANES - B_HEADS))).astype(BF16)
    b_f = jnp.pad(b_forget.astype(F32), (0, LANES - B_HEADS))[None]
    scale = LOG2E / math.sqrt(HEAD_DIM)
    scale_m = LOG2E / math.sqrt(M_HEAD_DIM)
    ones = lambda n: jnp.ones((n,), F32)
    gain = jnp.concatenate([
        jnp.tile(q_norm_a.astype(F32) * scale, A_HEADS), jnp.tile(k_norm_a.astype(F32), A_HEADS), ones(2 * A_WIDTH),
        jnp.tile(q_norm_b.astype(F32) * scale, B_HEADS), jnp.tile(k_norm_b.astype(F32), B_HEADS), ones(2 * B_WIDTH),
        jnp.tile(q_norm_m.astype(F32) * scale_m, M_HEADS), ones(PROJ_COLS - COL_ZM * LANES)])[None]

    h, logf = _norm(x2, norm_g.astype(F32)[None], w_f, b_f)
    proj = _proj(h, w_in.astype(F32), layer, gain)

    def logit_bound(gq, gk, dim, scl):
        return dim * scl * jnp.max(jnp.abs(gq.astype(F32))) * jnp.max(jnp.abs(gk.astype(F32)))

    def dispatch(bound, call, *operands):
        return lax.cond(bound <= LOGIT_RANGE, functools.partial(call, bounded=True),
                        functools.partial(call, bounded=False), *operands)

    bound_a = logit_bound(q_norm_a, k_norm_a, HEAD_DIM, scale) + LOG2E * jnp.max(jnp.abs(rel_bias.astype(F32)))
    oa = dispatch(bound_a, lambda p, b, bounded: _attn_a(p, b, batch, bounded),
                  proj, _bias_tiles(rel_bias.astype(F32)))

    c, ccol = _cumsum(logf, batch)
    ob = dispatch(logit_bound(q_norm_b, k_norm_b, HEAD_DIM, scale),
                  lambda p, c_, cc, bounded: _attn_b(p, c_, cc, batch, bounded), proj, c, ccol)

    kv = _mem_kv(mem2, mem_norm_g.astype(F32)[None], w_mem_kv.astype(BF16),
                 jnp.tile(k_norm_m.astype(F32), M_HEADS)[None], batch)
    om = dispatch(logit_bound(q_norm_m, k_norm_m, M_HEAD_DIM, scale_m),
                  lambda p, kv_, bounded: _attn_m(p, kv_, batch, bounded), proj, kv)

    return _merge(x2, oa, ob, om, proj, b_gate.astype(F32).reshape(1, N_BRANCH * D_MODEL),
                  w_proj_a.astype(BF16), w_proj_b.astype(BF16), w_proj_m.astype(BF16), w_out.astype(BF16))


def kernel(x, mem, norm_g, mem_norm_g, w_in, b_forget, b_gate, rel_bias, q_norm_a, k_norm_a, q_norm_b, k_norm_b,
           q_norm_m, k_norm_m, w_mem_kv, w_proj_a, w_proj_b, w_proj_m, w_out):
    batch, seq, d_model = x.shape
    assert (seq, d_model) == (SEQ, D_MODEL) and mem.shape == (batch, MEM_LEN, D_MODEL)
    x2 = x.reshape(batch * seq, d_model)
    mem2 = mem.reshape(batch * MEM_LEN, d_model)
    depth = w_in.shape[0]
    for l in range(depth):
        x2 = _layer(x2, mem2, batch, l, norm_g[l], mem_norm_g[l], w_in, b_forget[l], b_gate[l], rel_bias,
                    q_norm_a[l], k_norm_a[l], q_norm_b[l], k_norm_b[l], q_norm_m[l], k_norm_m[l],
                    w_mem_kv[l], w_proj_a[l], w_proj_b[l], w_proj_m[l], w_out[l])
    return x2.reshape(batch, seq, d_model)
```

```python
import functools
import math

import numpy as np
import jax
import jax.numpy as jnp
from jax import lax
from jax.experimental import pallas as pl
from jax.experimental.pallas import tpu as pltpu

F32 = jnp.float32
BF16 = jnp.bfloat16

D_MODEL = 2048
SEQ = 2048
HEAD_DIM = 128
A_HEADS = 12
B_HEADS = 8
M_HEADS = 4
M_HEAD_DIM = 256
MEM_LEN = 256
A_WIDTH = A_HEADS * HEAD_DIM
B_WIDTH = B_HEADS * HEAD_DIM
M_WIDTH = M_HEADS * M_HEAD_DIM
N_BRANCH = 3
BLK = 128
NBLK = SEQ // BLK
DILATED_PATTERNS = ((128, 1), (512, 4), (2048, 16))
REL_BUCKETS = 32
REL_MAX_DIST = 2048
EPS = 1e-6
NEG = -1e30
LOG2E = math.log2(math.e)

LANES = 128
VMEM_LIMIT = 56 * 1024 * 1024

COL_QA = 0
COL_KA = COL_QA + A_HEADS
COL_VA = COL_KA + A_HEADS
COL_ZA = COL_VA + A_HEADS
COL_QB = COL_ZA + A_HEADS
COL_KB = COL_QB + B_HEADS
COL_VB = COL_KB + B_HEADS
COL_ZB = COL_VB + B_HEADS
COL_QM = COL_ZB + B_HEADS
COL_ZM = COL_QM + M_WIDTH // LANES
COL_GL = COL_ZM + M_WIDTH // LANES
COL_END = COL_GL + N_BRANCH * D_MODEL // LANES
PROJ_COLS = COL_END * LANES

NORM_TM = 1024
PROJ_TM = 2048
PROJ_TN = 1024
PROJ_CHUNK = 256
PROJ_NB_HEAD = COL_QM * LANES // PROJ_TN
F_SHIFT = B_HEADS


_NT = (((1,), (1,)), ((), ()))


def _log_sigmoid(x):
    return jnp.minimum(x, 0.0) - jnp.log1p(jnp.exp(-jnp.abs(x)))


def _rms_scale(a):
    return a * lax.rsqrt(jnp.mean(a * a, axis=-1, keepdims=True) + EPS)


def _emit_pipelined(tasks, lookahead):
    pending = []
    for start, finish in tasks:
        pending.append((finish, start()))
        if len(pending) > lookahead:
            fin, state = pending.pop(0)
            fin(*state)
    for fin, state in pending:
        fin(*state)


def _col_block_range(col0, width):
    return col0 * LANES // PROJ_TN, (col0 * LANES + width) // PROJ_TN


def _in_col_blocks(j, *ranges):
    hit = None
    for lo, hi in ranges:
        cond = (j >= lo) & (j < hi)
        hit = cond if hit is None else hit | cond
    return hit


def _norm_body(x_ref, g_ref, wf_ref, bf_ref, h_ref, f_ref):
    hb = (_rms_scale(x_ref[...]) * g_ref[...]).astype(BF16)
    h_ref[...] = hb
    wf = jnp.concatenate([wf_ref[...], jnp.zeros((LANES - B_HEADS, D_MODEL), F32)], axis=0).astype(BF16)
    f = lax.dot_general(hb, wf, _NT, preferred_element_type=F32) + bf_ref[...]
    f_ref[...] = _log_sigmoid(f)


def _norm(x2, g, w_t, layer, bfp):
    rows = x2.shape[0]
    return pl.pallas_call(
        _norm_body,
        grid=(rows // NORM_TM,),
        in_specs=[
            pl.BlockSpec((NORM_TM, D_MODEL), lambda i: (i, 0)),
            pl.BlockSpec((1, D_MODEL), lambda i: (0, 0)),
            pl.BlockSpec((None, B_HEADS, D_MODEL), lambda i: (layer, COL_QM * LANES // B_HEADS, 0)),
            pl.BlockSpec((1, LANES), lambda i: (0, 0)),
        ],
        out_specs=[pl.BlockSpec((NORM_TM, D_MODEL), lambda i: (i, 0)),
                   pl.BlockSpec((NORM_TM, LANES), lambda i: (i, 0))],
        out_shape=[jax.ShapeDtypeStruct((rows, D_MODEL), BF16), jax.ShapeDtypeStruct((rows, LANES), F32)],
        compiler_params=pltpu.CompilerParams(dimension_semantics=("parallel",), vmem_limit_bytes=VMEM_LIMIT),
        name="norm",
    )(x2, g, w_t, bfp)


def _proj_body(h_ref, wa_ref, wb_ref, gain_ref, out_ref, w16_ref):
    j = pl.program_id(0)

    @pl.when((pl.program_id(1) == 0) & (j < PROJ_NB_HEAD))
    def _():
        w16_ref[...] = wa_ref[...].astype(BF16)

    @pl.when((pl.program_id(1) == 0) & (j >= PROJ_NB_HEAD))
    def _():
        w16_ref[...] = jnp.concatenate([wa_ref[F_SHIFT:, :], wb_ref[...]], axis=0).astype(BF16)

    def plain(acc, rows):
        out_ref[rows, :] = acc.astype(BF16)

    def head_normed(hd):
        def epilogue(acc, rows):
            gain = gain_ref[...]
            for s in range(PROJ_TN // hd):
                sl = slice(s * hd, (s + 1) * hd)
                out_ref[rows, sl] = (_rms_scale(acc[:, sl]) * gain[:, sl]).astype(BF16)
        return epilogue

    def run(epilogue):
        tasks = []
        for r in range(PROJ_TM // PROJ_CHUNK):
            rows = slice(r * PROJ_CHUNK, (r + 1) * PROJ_CHUNK)
            tasks.append((lambda rows=rows: (lax.dot_general(h_ref[rows, :], w16_ref[...], _NT,
                                                             preferred_element_type=F32),),
                          lambda acc, rows=rows: epilogue(acc, rows)))
        _emit_pipelined(tasks, 1)

    head128 = _in_col_blocks(j, _col_block_range(COL_QA, 2 * A_WIDTH), _col_block_range(COL_QB, 2 * B_WIDTH))
    head256 = _in_col_blocks(j, _col_block_range(COL_QM, M_WIDTH))
    pl.when(head128)(lambda: run(head_normed(HEAD_DIM)))
    pl.when(head256)(lambda: run(head_normed(M_HEAD_DIM)))
    pl.when(jnp.logical_not(head128 | head256))(lambda: run(plain))


def _proj(h, w_t, layer, gain):
    rows = h.shape[0]

    def w_main(j, i):
        return layer, j, 0

    def w_spill(j, i):
        return layer, (jnp.maximum(j, PROJ_NB_HEAD) + 1) * (PROJ_TN // F_SHIFT), 0

    return pl.pallas_call(
        _proj_body,
        grid=(PROJ_COLS // PROJ_TN, rows // PROJ_TM),
        in_specs=[
            pl.BlockSpec((PROJ_TM, D_MODEL), lambda j, i: (i, 0)),
            pl.BlockSpec((None, PROJ_TN, D_MODEL), w_main),
            pl.BlockSpec((None, F_SHIFT, D_MODEL), w_spill),
            pl.BlockSpec((1, PROJ_TN), lambda j, i: (0, j)),
        ],
        out_specs=pl.BlockSpec((PROJ_TM, PROJ_TN), lambda j, i: (i, j)),
        out_shape=jax.ShapeDtypeStruct((rows, PROJ_COLS), BF16),
        scratch_shapes=[pltpu.VMEM((PROJ_TN, D_MODEL), BF16)],
        compiler_params=pltpu.CompilerParams(
            dimension_semantics=("parallel", "arbitrary"), vmem_limit_bytes=VMEM_LIMIT),
        name="proj",
    )(h, w_t, w_t, gain)


def _rel_bucket_np(dist):
    max_exact = REL_BUCKETS // 2
    d_f = np.maximum(dist, 1).astype(np.float32)
    large = max_exact + (np.log(d_f / np.float32(max_exact)) / np.float32(math.log(REL_MAX_DIST / max_exact))
                         * np.float32(REL_BUCKETS - max_exact)).astype(np.int32)
    large = np.minimum(large, REL_BUCKETS - 1)
    return np.where(dist < max_exact, dist, large).astype(np.int32)


def _bias_buckets():
    qi = np.arange(BLK)[:, None]
    kj = np.arange(2 * BLK)[None, :]
    delta = qi - kj + BLK
    tiles = []
    for window, dil in DILATED_PATTERNS:
        w_sub = window // dil
        valid = (delta >= 0) & (delta <= w_sub)
        tiles.append(np.where(valid, _rel_bucket_np(np.clip(delta, 0, w_sub) * dil), -1))
    return np.stack(tiles).astype(np.int32)


def _bias_body(tbl_ref, bucket_ref, out_ref):
    bucket = bucket_ref[...]
    for h in range(A_HEADS):
        out = jnp.full(bucket.shape, NEG, F32)
        for r in range(REL_BUCKETS):
            out = jnp.where(bucket == r, tbl_ref[r, h] * LOG2E, out)
        out_ref[h] = out


def _bias_tiles(rel_bias):
    n_pat = len(DILATED_PATTERNS)
    return pl.pallas_call(
        _bias_body,
        grid=(n_pat,),
        in_specs=[pl.BlockSpec(memory_space=pltpu.SMEM),
                  pl.BlockSpec((None, BLK, 2 * BLK), lambda p: (p, 0, 0))],
        out_specs=pl.BlockSpec((None, A_HEADS, BLK, 2 * BLK), lambda p: (p, 0, 0, 0)),
        out_shape=jax.ShapeDtypeStruct((n_pat, A_HEADS, BLK, 2 * BLK), F32),
        name="rel_bias_tiles",
    )(rel_bias, jnp.asarray(_bias_buckets()))


def _cumsum_body(f_ref, c_ref, ccol_ref):
    ft = f_ref[...].T[:B_HEADS, :]
    pos = lax.broadcasted_iota(jnp.int32, ft.shape, 1)
    shift = 1
    while shift < SEQ:
        ft = ft + jnp.where(pos >= shift, pltpu.roll(ft, shift, axis=1), 0.0)
        shift *= 2
    c = ft * LOG2E
    c_ref[...] = c
    ccol_ref[...] = jnp.concatenate([c, jnp.zeros((LANES - B_HEADS, SEQ), F32)], axis=0).T


def _cumsum(logf, batch):
    return pl.pallas_call(
        _cumsum_body,
        grid=(batch,),
        in_specs=[pl.BlockSpec((SEQ, LANES), lambda b: (b, 0))],
        out_specs=[pl.BlockSpec((None, B_HEADS, SEQ), lambda b: (b, 0, 0)),
                   pl.BlockSpec((SEQ, LANES), lambda b: (b, 0))],
        out_shape=[jax.ShapeDtypeStruct((batch, B_HEADS, SEQ), F32),
                   jax.ShapeDtypeStruct((batch * SEQ, LANES), F32)],
        name="forget_cumsum",
    )(logf)


P3_PITCH = BLK + 8


def _scores(q, k, bias):
    return lax.dot_general(q, k, _NT, preferred_element_type=F32) + bias


def _softmax_pv(s, v, bounded):
    m = 0.0 if bounded else jnp.max(s, axis=-1, keepdims=True)
    pe = jnp.exp2(s) if bounded else jnp.exp2(s - m)
    l = jnp.sum(pe, axis=-1, keepdims=True)
    return jnp.dot(pe.astype(BF16), v, preferred_element_type=F32), l, m


LOGIT_RANGE = 60.0
ATTN_A_LOOKAHEAD = 8


def _attn_a_body(bounded, q_ref, k_ref, v_ref, bias_ref, o_ref, nat, cm4, res2, res3):
    dil4, dil16 = DILATED_PATTERNS[1][1], DILATED_PATTERNS[2][1]
    cls_len = SEQ // dil4
    for i, ref in enumerate((q_ref, k_ref, v_ref)):
        nat[i] = ref[...].astype(F32)
        for c in range(dil4):
            cm4[i, c * cls_len:(c + 1) * cls_len, :] = nat[i, pl.ds(c, cls_len, stride=dil4), :]

    def operands(q_rows, kv_rows):
        return (cm4[0, q_rows, :].astype(BF16), cm4[1, kv_rows, :].astype(BF16), cm4[2, kv_rows, :].astype(BF16))

    def save(res, rows, acc, l, m):
        res[0, rows, :] = acc
        res[1, rows, :] = jnp.broadcast_to(l, acc.shape)
        if not bounded:
            res[2, rows, :] = jnp.broadcast_to(m, acc.shape)

    tasks = []

    def add_task(q_rows, kv_rows, bias, res, out_rows):
        def start():
            q, k, v = operands(q_rows, kv_rows)
            return _scores(q, k, bias()), v

        tasks.append((start, lambda s, v: save(res, out_rows, *_softmax_pv(s, v, bounded))))

    for c in range(dil4):
        for n in range(cls_len // BLK):
            lo = c * cls_len + n * BLK
            out_rows = pl.ds(n * BLK * dil4 + c, BLK, stride=dil4)
            if n == 0:
                add_task(slice(lo, lo + BLK), slice(lo, lo + BLK), lambda: bias_ref[1, :, BLK:], res2, out_rows)
            else:
                add_task(slice(lo, lo + BLK), slice(lo - BLK, lo + BLK), lambda: bias_ref[1], res2, out_rows)

    for c in range(dil16):
        rows = pl.ds((c % dil4) * cls_len + c // dil4, BLK, stride=dil16 // dil4)
        add_task(rows, rows, lambda: bias_ref[2, :, BLK:], res3, slice(c * P3_PITCH, c * P3_PITCH + BLK))

    def token_order(i, g):
        per_class = BLK // dil16
        return jnp.concatenate(
            [res3[i, pl.ds(g * per_class + j, dil16, stride=P3_PITCH), :] for j in range(per_class)], axis=0)

    def add_merge_task(g):
        rows = slice(g * BLK, (g + 1) * BLK)
        keys = rows if g == 0 else slice((g - 1) * BLK, (g + 1) * BLK)

        def start():
            bias = bias_ref[0, :, BLK:] if g == 0 else bias_ref[0]
            return _scores(q_ref[rows, :], k_ref[keys, :], bias), v_ref[keys, :]

        def finish(s, v):
            acc1, l1, m1 = _softmax_pv(s, v, bounded)
            if bounded:
                num = acc1 + res2[0, rows, :] + token_order(0, g)
                den = l1 + res2[1, rows, :] + token_order(1, g)
            else:
                m2, m3 = res2[2, rows, :], token_order(2, g)
                top = jnp.maximum(jnp.maximum(m1, m2), m3)
                w1, w2, w3 = jnp.exp2(m1 - top), jnp.exp2(m2 - top), jnp.exp2(m3 - top)
                num = w1 * acc1 + w2 * res2[0, rows, :] + w3 * token_order(0, g)
                den = w1 * l1 + w2 * res2[1, rows, :] + w3 * token_order(1, g)
            o_ref[rows, :] = (num * (1.0 / den)).astype(BF16)

        tasks.append((start, finish))

    for g in range(NBLK):
        add_merge_task(g)
    _emit_pipelined(tasks, ATTN_A_LOOKAHEAD)


def _attn_a(proj, bias, batch, bounded):
    n_pat = len(DILATED_PATTERNS)
    head = lambda col0: pl.BlockSpec((SEQ, HEAD_DIM), lambda b, h: (b, col0 + h))
    p3_rows = DILATED_PATTERNS[2][1] * P3_PITCH
    n_stats = 2 if bounded else 3
    return pl.pallas_call(
        functools.partial(_attn_a_body, bounded),
        grid=(batch, A_HEADS),
        in_specs=[head(COL_QA), head(COL_KA), head(COL_VA),
                  pl.BlockSpec((n_pat, None, BLK, 2 * BLK), lambda b, h: (0, h, 0, 0))],
        out_specs=pl.BlockSpec((SEQ, HEAD_DIM), lambda b, h: (b, h)),
        out_shape=jax.ShapeDtypeStruct((batch * SEQ, A_WIDTH), BF16),
        scratch_shapes=[pltpu.VMEM((3, SEQ, HEAD_DIM), F32)] * 2 + [
            pltpu.VMEM((n_stats, SEQ, HEAD_DIM), F32), pltpu.VMEM((n_stats, p3_rows, HEAD_DIM), F32)],
        compiler_params=pltpu.CompilerParams(
            dimension_semantics=("parallel", "parallel"), vmem_limit_bytes=VMEM_LIMIT),
        name="attn_dilated_bounded" if bounded else "attn_dilated",
    )(proj, proj, proj, bias)


FOX_TQ = 256


def _attn_b_body(bounded, q_ref, k_ref, v_ref, c_ref, ccol_ref, o_ref):
    h = pl.program_id(1)
    ck = c_ref[pl.ds(h, 1), :]
    qi = lax.broadcasted_iota(jnp.int32, (FOX_TQ, FOX_TQ), 0)
    kj = lax.broadcasted_iota(jnp.int32, (FOX_TQ, FOX_TQ), 1)
    causal = kj <= qi
    head_lane = lax.broadcasted_iota(jnp.int32, (FOX_TQ, LANES), 1) == h

    def task(i):
        lo, hi = i * FOX_TQ, (i + 1) * FOX_TQ

        def start():
            q = q_ref[lo:hi, :]
            s_diag = lax.dot_general(q, k_ref[lo:hi, :], _NT, preferred_element_type=F32)
            s_off = lax.dot_general(q, k_ref[:lo, :], _NT, preferred_element_type=F32) if i > 0 else None
            return s_diag, s_off

        def finish(s_diag, s_off):
            if bounded:
                cq = jnp.sum(jnp.where(head_lane, ccol_ref[lo:hi, :], 0.0), axis=-1, keepdims=True)
                p_diag = jnp.exp2(jnp.where(causal, (s_diag + cq) - ck[:, lo:hi], NEG))
                p_off = jnp.exp2((s_off + cq) - ck[:, :lo]) if i > 0 else None
            else:
                s_diag = jnp.where(causal, s_diag - ck[:, lo:hi], NEG)
                m = jnp.max(s_diag, axis=-1, keepdims=True)
                if i > 0:
                    s_off = s_off - ck[:, :lo]
                    m = jnp.maximum(m, jnp.max(s_off, axis=-1, keepdims=True))
                p_diag = jnp.exp2(s_diag - m)
                p_off = jnp.exp2(s_off - m) if i > 0 else None
            l = jnp.sum(p_diag, axis=-1, keepdims=True)
            acc = jnp.dot(p_diag.astype(BF16), v_ref[lo:hi, :], preferred_element_type=F32)
            if i > 0:
                l = l + jnp.sum(p_off, axis=-1, keepdims=True)
                acc = acc + jnp.dot(p_off.astype(BF16), v_ref[:lo, :], preferred_element_type=F32)
            o_ref[lo:hi, :] = (acc * (1.0 / l)).astype(BF16)

        return start, finish

    _emit_pipelined([task(i) for i in range(SEQ // FOX_TQ)], 2)


def _attn_b(proj, c, ccol, batch, bounded):
    head = lambda col0: pl.BlockSpec((SEQ, HEAD_DIM), lambda b, h: (b, col0 + h))
    return pl.pallas_call(
        functools.partial(_attn_b_body, bounded),
        grid=(batch, B_HEADS),
        in_specs=[head(COL_QB), head(COL_KB), head(COL_VB),
                  pl.BlockSpec((None, B_HEADS, SEQ), lambda b, h: (b, 0, 0)),
                  pl.BlockSpec((SEQ, LANES), lambda b, h: (b, 0))],
        out_specs=pl.BlockSpec((SEQ, HEAD_DIM), lambda b, h: (b, h)),
        out_shape=jax.ShapeDtypeStruct((batch * SEQ, B_WIDTH), BF16),
        compiler_params=pltpu.CompilerParams(
            dimension_semantics=("parallel", "parallel"), vmem_limit_bytes=VMEM_LIMIT),
        name="attn_forgetting_bounded" if bounded else "attn_forgetting",
    )(proj, proj, proj, c, ccol)


def _mem_kv_body(mem_ref, g_ref, w_ref, gain_ref, out_ref):
    hb = (_rms_scale(mem_ref[...]) * g_ref[...]).astype(BF16)
    acc = jnp.dot(hb, w_ref[...], preferred_element_type=F32)
    gain = gain_ref[...]
    for s in range(M_HEADS):
        sl = slice(s * M_HEAD_DIM, (s + 1) * M_HEAD_DIM)
        out_ref[:, sl] = (_rms_scale(acc[:, sl]) * gain[:, sl]).astype(BF16)
    out_ref[:, M_WIDTH:] = acc[:, M_WIDTH:].astype(BF16)


def _mem_kv(mem2, g, w, gain, batch):
    return pl.pallas_call(
        _mem_kv_body,
        grid=(batch,),
        in_specs=[
            pl.BlockSpec((MEM_LEN, D_MODEL), lambda b: (b, 0)),
            pl.BlockSpec((1, D_MODEL), lambda b: (0, 0)),
            pl.BlockSpec((D_MODEL, 2 * M_WIDTH), lambda b: (0, 0)),
            pl.BlockSpec((1, M_WIDTH), lambda b: (0, 0)),
        ],
        out_specs=pl.BlockSpec((MEM_LEN, 2 * M_WIDTH), lambda b: (b, 0)),
        out_shape=jax.ShapeDtypeStruct((batch * MEM_LEN, 2 * M_WIDTH), BF16),
        compiler_params=pltpu.CompilerParams(
            dimension_semantics=("parallel",), vmem_limit_bytes=VMEM_LIMIT),
        name="mem_kv",
    )(mem2, g, w, gain)


MEM_TQ = 512


def _attn_m_body(bounded, q_ref, k_ref, v_ref, o_ref):
    def task(i):
        rows = slice(i * MEM_TQ, (i + 1) * MEM_TQ)

        def finish(s):
            acc, l, _ = _softmax_pv(s, v_ref[...], bounded)
            o_ref[rows, :] = (acc * (1.0 / l)).astype(BF16)

        return lambda: (lax.dot_general(q_ref[rows, :], k_ref[...], _NT, preferred_element_type=F32),), finish

    _emit_pipelined([task(i) for i in range(SEQ // MEM_TQ)], 1)


def _attn_m(proj, kv, batch, bounded):
    col_qm = COL_QM * LANES // M_HEAD_DIM
    return pl.pallas_call(
        functools.partial(_attn_m_body, bounded),
        grid=(batch, M_HEADS),
        in_specs=[
            pl.BlockSpec((SEQ, M_HEAD_DIM), lambda b, h: (b, col_qm + h)),
            pl.BlockSpec((MEM_LEN, M_HEAD_DIM), lambda b, h: (b, h)),
            pl.BlockSpec((MEM_LEN, M_HEAD_DIM), lambda b, h: (b, M_HEADS + h)),
        ],
        out_specs=pl.BlockSpec((SEQ, M_HEAD_DIM), lambda b, h: (b, h)),
        out_shape=jax.ShapeDtypeStruct((batch * SEQ, M_WIDTH), BF16),
        compiler_params=pltpu.CompilerParams(
            dimension_semantics=("parallel", "parallel"), vmem_limit_bytes=VMEM_LIMIT),
        name="attn_memory_bounded" if bounded else "attn_memory",
    )(proj, kv, kv)


MERGE_TM = 256


def _merge_body(x_ref, oa_ref, za_ref, ob_ref, zb_ref, om_ref, zm_ref, gl_ref, bg_ref,
                wa_ref, wb_ref, wm_ref, wo_ref, out_ref):
    def branch(o_ref, z_ref, w_ref):
        z = z_ref[...].astype(F32)
        gated = o_ref[...].astype(F32) * (z * jax.nn.sigmoid(z))
        return jnp.dot(gated.astype(BF16), w_ref[...], preferred_element_type=F32)

    merged = None
    for n, (o_ref, z_ref, w_ref) in enumerate(((oa_ref, za_ref, wa_ref), (ob_ref, zb_ref, wb_ref),
                                                (om_ref, zm_ref, wm_ref))):
        cols = slice(n * D_MODEL, (n + 1) * D_MODEL)
        gate = jax.nn.sigmoid(gl_ref[:, cols].astype(F32) + bg_ref[:, cols])
        term = gate * branch(o_ref, z_ref, w_ref)
        merged = term if merged is None else merged + term
    out_ref[...] = x_ref[...] + jnp.dot(merged.astype(BF16), wo_ref[...], preferred_element_type=F32)


def _merge(x2, oa, ob, om, proj, bg, wa, wb, wm, wo):
    rows = x2.shape[0]
    row_blk = lambda width, colblk: pl.BlockSpec((MERGE_TM, width), lambda i: (i, colblk))
    resident = lambda shape: pl.BlockSpec(shape, lambda i: (0, 0), pipeline_mode=pl.Buffered(1))
    assert (COL_ZA * LANES) % A_WIDTH == 0 and (COL_ZB * LANES) % B_WIDTH == 0
    assert (COL_ZM * LANES) % M_WIDTH == 0 and (COL_GL * LANES) % (N_BRANCH * D_MODEL) == 0
    return pl.pallas_call(
        _merge_body,
        grid=(rows // MERGE_TM,),
        in_specs=[
            row_blk(D_MODEL, 0),
            row_blk(A_WIDTH, 0), row_blk(A_WIDTH, COL_ZA * LANES // A_WIDTH),
            row_blk(B_WIDTH, 0), row_blk(B_WIDTH, COL_ZB * LANES // B_WIDTH),
            row_blk(M_WIDTH, 0), row_blk(M_WIDTH, COL_ZM * LANES // M_WIDTH),
            row_blk(N_BRANCH * D_MODEL, COL_GL * LANES // (N_BRANCH * D_MODEL)),
            resident((1, N_BRANCH * D_MODEL)),
            resident((A_WIDTH, D_MODEL)), resident((B_WIDTH, D_MODEL)), resident((M_WIDTH, D_MODEL)),
            resident((D_MODEL, D_MODEL)),
        ],
        out_specs=row_blk(D_MODEL, 0),
        out_shape=jax.ShapeDtypeStruct((rows, D_MODEL), F32),
        compiler_params=pltpu.CompilerParams(
            dimension_semantics=("parallel",), vmem_limit_bytes=VMEM_LIMIT),
        name="merge_out",
    )(x2, oa, proj, ob, proj, om, proj, proj, bg, wa, wb, wm, wo)


def _layer(x2, mem2, batch, layer, norm_g, mem_norm_g, w_in, b_forget, b_gate, rel_bias, q_norm_a, k_norm_a,
           q_norm_b, k_norm_b, q_norm_m, k_norm_m, w_mem_kv, w_proj_a, w_proj_b, w_proj_m, w_out):
    b_f = jnp.pad(b_forget.astype(F32), (0, LANES - B_HEADS))[None]
    scale = LOG2E / math.sqrt(HEAD_DIM)
    scale_m = LOG2E / math.sqrt(M_HEAD_DIM)
    ones = lambda n: jnp.ones((n,), F32)
    gain = jnp.concatenate([
        jnp.tile(q_norm_a.astype(F32) * scale, A_HEADS), jnp.tile(k_norm_a.astype(F32), A_HEADS), ones(2 * A_WIDTH),
        jnp.tile(q_norm_b.astype(F32) * scale, B_HEADS), jnp.tile(k_norm_b.astype(F32), B_HEADS), ones(2 * B_WIDTH),
        jnp.tile(q_norm_m.astype(F32) * scale_m, M_HEADS), ones(PROJ_COLS - COL_ZM * LANES)])[None]

    w_t = jnp.swapaxes(w_in.astype(F32), 1, 2)
    h, logf = _norm(x2, norm_g.astype(F32)[None], w_t, layer, b_f)
    proj = _proj(h, w_t, layer, gain)

    def logit_bound(gq, gk, dim, scl):
        return dim * scl * jnp.max(jnp.abs(gq.astype(F32))) * jnp.max(jnp.abs(gk.astype(F32)))

    def dispatch(bound, call, *operands):
        return lax.cond(bound <= LOGIT_RANGE, functools.partial(call, bounded=True),
                        functools.partial(call, bounded=False), *operands)

    bound_a = logit_bound(q_norm_a, k_norm_a, HEAD_DIM, scale) + LOG2E * jnp.max(jnp.abs(rel_bias.astype(F32)))
    oa = dispatch(bound_a, lambda p, b, bounded: _attn_a(p, b, batch, bounded),
                  proj, _bias_tiles(rel_bias.astype(F32)))

    c, ccol = _cumsum(logf, batch)
    ob = dispatch(logit_bound(q_norm_b, k_norm_b, HEAD_DIM, scale),
                  lambda p, c_, cc, bounded: _attn_b(p, c_, cc, batch, bounded), proj, c, ccol)

    kv = _mem_kv(mem2, mem_norm_g.astype(F32)[None], w_mem_kv.astype(BF16),
                 jnp.tile(k_norm_m.astype(F32), M_HEADS)[None], batch)
    om = dispatch(logit_bound(q_norm_m, k_norm_m, M_HEAD_DIM, scale_m),
                  lambda p, kv_, bounded: _attn_m(p, kv_, batch, bounded), proj, kv)

    return _merge(x2, oa, ob, om, proj, b_gate.astype(F32).reshape(1, N_BRANCH * D_MODEL),
                  w_proj_a.astype(BF16), w_proj_b.astype(BF16), w_proj_m.astype(BF16), w_out.astype(BF16))


def kernel(x, mem, norm_g, mem_norm_g, w_in, b_forget, b_gate, rel_bias, q_norm_a, k_norm_a, q_norm_b, k_norm_b,
           q_norm_m, k_norm_m, w_mem_kv, w_proj_a, w_proj_b, w_proj_m, w_out):
    batch, seq, d_model = x.shape
    assert (seq, d_model) == (SEQ, D_MODEL) and mem.shape == (batch, MEM_LEN, D_MODEL)
    x2 = x.reshape(batch * seq, d_model)
    mem2 = mem.reshape(batch * MEM_LEN, d_model)
    depth = w_in.shape[0]
    for l in range(depth):
        x2 = _layer(x2, mem2, batch, l, norm_g[l], mem_norm_g[l], w_in, b_forget[l], b_gate[l], rel_bias,
                    q_norm_a[l], k_norm_a[l], q_norm_b[l], k_norm_b[l], q_norm_m[l], k_norm_m[l],
                    w_mem_kv[l], w_proj_a[l], w_proj_b[l], w_proj_m[l], w_out[l])
    return x2.reshape(batch, seq, d_model)
```

```python
import functools
import math

import numpy as np
import jax
import jax.numpy as jnp
from jax import lax
from jax.experimental import pallas as pl
from jax.experimental.pallas import tpu as pltpu

F32 = jnp.float32
BF16 = jnp.bfloat16

D_MODEL = 2048
SEQ = 2048
HEAD_DIM = 128
A_HEADS = 12
B_HEADS = 8
M_HEADS = 4
M_HEAD_DIM = 256
MEM_LEN = 256
A_WIDTH = A_HEADS * HEAD_DIM
B_WIDTH = B_HEADS * HEAD_DIM
M_WIDTH = M_HEADS * M_HEAD_DIM
N_BRANCH = 3
BLK = 128
NBLK = SEQ // BLK
DILATED_PATTERNS = ((128, 1), (512, 4), (2048, 16))
REL_BUCKETS = 32
REL_MAX_DIST = 2048
EPS = 1e-6
NEG = -1e30
LOG2E = math.log2(math.e)

LANES = 128
VMEM_LIMIT = 56 * 1024 * 1024

COL_QA = 0
COL_KA = COL_QA + A_HEADS
COL_VA = COL_KA + A_HEADS
COL_ZA = COL_VA + A_HEADS
COL_QB = COL_ZA + A_HEADS
COL_KB = COL_QB + B_HEADS
COL_VB = COL_KB + B_HEADS
COL_ZB = COL_VB + B_HEADS
COL_QM = COL_ZB + B_HEADS
COL_ZM = COL_QM + M_WIDTH // LANES
COL_GL = COL_ZM + M_WIDTH // LANES
COL_END = COL_GL + N_BRANCH * D_MODEL // LANES
PROJ_COLS = COL_END * LANES

NORM_TM = 1024
PROJ_TM = 2048
PROJ_TN = 1024
PROJ_CHUNK = 256
PROJ_NB_HEAD = COL_QM * LANES // PROJ_TN
F_SHIFT = B_HEADS


_NT = (((1,), (1,)), ((), ()))


def _log_sigmoid(x):
    return jnp.minimum(x, 0.0) - jnp.log1p(jnp.exp(-jnp.abs(x)))


def _rms_scale(a):
    return a * lax.rsqrt(jnp.mean(a * a, axis=-1, keepdims=True) + EPS)


def _emit_pipelined(tasks, lookahead):
    pending = []
    for start, finish in tasks:
        pending.append((finish, start()))
        if len(pending) > lookahead:
            fin, state = pending.pop(0)
            fin(*state)
    for fin, state in pending:
        fin(*state)


def _col_block_range(col0, width):
    return col0 * LANES // PROJ_TN, (col0 * LANES + width) // PROJ_TN


def _in_col_blocks(j, *ranges):
    hit = None
    for lo, hi in ranges:
        cond = (j >= lo) & (j < hi)
        hit = cond if hit is None else hit | cond
    return hit


def _head_normed(out_ref, gain_ref, hd):
    def epilogue(acc, rows):
        gain = gain_ref[...]
        for s in range(PROJ_TN // hd):
            sl = slice(s * hd, (s + 1) * hd)
            out_ref[rows, sl] = (_rms_scale(acc[:, sl]) * gain[:, sl]).astype(BF16)
    return epilogue


def _norm_body(x_ref, g_ref, wf_ref, bf_ref, w0_ref, gain_ref, h_ref, f_ref, p_ref, w16_ref):
    @pl.when(pl.program_id(0) == 0)
    def _():
        w16_ref[...] = w0_ref[...].astype(BF16)

    wf = jnp.concatenate([wf_ref[...], jnp.zeros((LANES - B_HEADS, D_MODEL), F32)], axis=0).astype(BF16)
    epilogue = _head_normed(p_ref, gain_ref, HEAD_DIM)
    tasks = []
    for r in range(NORM_TM // PROJ_CHUNK):
        rows = slice(r * PROJ_CHUNK, (r + 1) * PROJ_CHUNK)

        def start(rows=rows):
            hb = (_rms_scale(x_ref[rows, :]) * g_ref[...]).astype(BF16)
            h_ref[rows, :] = hb
            f = lax.dot_general(hb, wf, _NT, preferred_element_type=F32) + bf_ref[...]
            f_ref[rows, :] = _log_sigmoid(f)
            return (lax.dot_general(hb, w16_ref[...], _NT, preferred_element_type=F32),)

        tasks.append((start, lambda acc, rows=rows: epilogue(acc, rows)))
    _emit_pipelined(tasks, 2)


def _norm(x2, g, w_t, layer, bfp, gain):
    rows = x2.shape[0]
    assert _col_block_range(COL_QA, 2 * A_WIDTH)[0] == 0
    return pl.pallas_call(
        _norm_body,
        grid=(rows // NORM_TM,),
        in_specs=[
            pl.BlockSpec((NORM_TM, D_MODEL), lambda i: (i, 0)),
            pl.BlockSpec((1, D_MODEL), lambda i: (0, 0)),
            pl.BlockSpec((None, B_HEADS, D_MODEL), lambda i: (layer, COL_QM * LANES // B_HEADS, 0)),
            pl.BlockSpec((1, LANES), lambda i: (0, 0)),
            pl.BlockSpec((None, PROJ_TN, D_MODEL), lambda i: (layer, 0, 0), pipeline_mode=pl.Buffered(1)),
            pl.BlockSpec((1, PROJ_TN), lambda i: (0, 0)),
        ],
        out_specs=[pl.BlockSpec((NORM_TM, D_MODEL), lambda i: (i, 0)),
                   pl.BlockSpec((NORM_TM, LANES), lambda i: (i, 0)),
                   pl.BlockSpec((NORM_TM, PROJ_TN), lambda i: (i, 0))],
        out_shape=[jax.ShapeDtypeStruct((rows, D_MODEL), BF16), jax.ShapeDtypeStruct((rows, LANES), F32),
                   jax.ShapeDtypeStruct((rows, PROJ_COLS), BF16)],
        scratch_shapes=[pltpu.VMEM((PROJ_TN, D_MODEL), BF16)],
        compiler_params=pltpu.CompilerParams(dimension_semantics=("arbitrary",), vmem_limit_bytes=VMEM_LIMIT),
        name="norm_proj0",
    )(x2, g, w_t, bfp, w_t, gain)


def _proj_body(h_ref, wa_ref, wb_ref, gain_ref, proj0_ref, out_ref, w16_ref):
    del proj0_ref
    j = pl.program_id(0) + 1

    @pl.when((pl.program_id(1) == 0) & (j < PROJ_NB_HEAD))
    def _():
        w16_ref[...] = wa_ref[...].astype(BF16)

    @pl.when((pl.program_id(1) == 0) & (j >= PROJ_NB_HEAD))
    def _():
        w16_ref[...] = jnp.concatenate([wa_ref[F_SHIFT:, :], wb_ref[...]], axis=0).astype(BF16)

    def plain(acc, rows):
        out_ref[rows, :] = acc.astype(BF16)

    def run(epilogue):
        tasks = []
        for r in range(PROJ_TM // PROJ_CHUNK):
            rows = slice(r * PROJ_CHUNK, (r + 1) * PROJ_CHUNK)
            tasks.append((lambda rows=rows: (lax.dot_general(h_ref[rows, :], w16_ref[...], _NT,
                                                             preferred_element_type=F32),),
                          lambda acc, rows=rows: epilogue(acc, rows)))
        _emit_pipelined(tasks, 1)

    head128 = _in_col_blocks(j, _col_block_range(COL_QA, 2 * A_WIDTH), _col_block_range(COL_QB, 2 * B_WIDTH))
    head256 = _in_col_blocks(j, _col_block_range(COL_QM, M_WIDTH))
    pl.when(head128)(lambda: run(_head_normed(out_ref, gain_ref, HEAD_DIM)))
    pl.when(head256)(lambda: run(_head_normed(out_ref, gain_ref, M_HEAD_DIM)))
    pl.when(jnp.logical_not(head128 | head256))(lambda: run(plain))


def _proj(h, w_t, layer, gain, proj0):
    rows = h.shape[0]

    def w_main(j, i):
        return layer, j + 1, 0

    def w_spill(j, i):
        return layer, (jnp.maximum(j + 1, PROJ_NB_HEAD) + 1) * (PROJ_TN // F_SHIFT), 0

    return pl.pallas_call(
        _proj_body,
        grid=(PROJ_COLS // PROJ_TN - 1, rows // PROJ_TM),
        in_specs=[
            pl.BlockSpec((PROJ_TM, D_MODEL), lambda j, i: (i, 0)),
            pl.BlockSpec((None, PROJ_TN, D_MODEL), w_main),
            pl.BlockSpec((None, F_SHIFT, D_MODEL), w_spill),
            pl.BlockSpec((1, PROJ_TN), lambda j, i: (0, j + 1)),
            pl.BlockSpec(memory_space=pl.ANY),
        ],
        out_specs=pl.BlockSpec((PROJ_TM, PROJ_TN), lambda j, i: (i, j + 1)),
        out_shape=jax.ShapeDtypeStruct((rows, PROJ_COLS), BF16),
        input_output_aliases={4: 0},
        scratch_shapes=[pltpu.VMEM((PROJ_TN, D_MODEL), BF16)],
        compiler_params=pltpu.CompilerParams(
            dimension_semantics=("parallel", "arbitrary"), vmem_limit_bytes=VMEM_LIMIT),
        name="proj",
    )(h, w_t, w_t, gain, proj0)


def _rel_bucket_np(dist):
    max_exact = REL_BUCKETS // 2
    d_f = np.maximum(dist, 1).astype(np.float32)
    large = max_exact + (np.log(d_f / np.float32(max_exact)) / np.float32(math.log(REL_MAX_DIST / max_exact))
                         * np.float32(REL_BUCKETS - max_exact)).astype(np.int32)
    large = np.minimum(large, REL_BUCKETS - 1)
    return np.where(dist < max_exact, dist, large).astype(np.int32)


def _bias_buckets():
    qi = np.arange(BLK)[:, None]
    kj = np.arange(2 * BLK)[None, :]
    delta = qi - kj + BLK
    tiles = []
    for window, dil in DILATED_PATTERNS:
        w_sub = window // dil
        valid = (delta >= 0) & (delta <= w_sub)
        tiles.append(np.where(valid, _rel_bucket_np(np.clip(delta, 0, w_sub) * dil), -1))
    return np.stack(tiles).astype(np.int32)


def _bias_body(tbl_ref, bucket_ref, out_ref):
    bucket = bucket_ref[...]
    for h in range(A_HEADS):
        out = jnp.full(bucket.shape, NEG, F32)
        for r in range(REL_BUCKETS):
            out = jnp.where(bucket == r, tbl_ref[r, h] * LOG2E, out)
        out_ref[h] = out


def _bias_tiles(rel_bias):
    n_pat = len(DILATED_PATTERNS)
    return pl.pallas_call(
        _bias_body,
        grid=(n_pat,),
        in_specs=[pl.BlockSpec(memory_space=pltpu.SMEM),
                  pl.BlockSpec((None, BLK, 2 * BLK), lambda p: (p, 0, 0))],
        out_specs=pl.BlockSpec((None, A_HEADS, BLK, 2 * BLK), lambda p: (p, 0, 0, 0)),
        out_shape=jax.ShapeDtypeStruct((n_pat, A_HEADS, BLK, 2 * BLK), F32),
        name="rel_bias_tiles",
    )(rel_bias, jnp.asarray(_bias_buckets()))


def _cumsum_body(f_ref, c_ref, ccol_ref):
    ft = f_ref[...].T[:B_HEADS, :]
    pos = lax.broadcasted_iota(jnp.int32, ft.shape, 1)
    shift = 1
    while shift < SEQ:
        ft = ft + jnp.where(pos >= shift, pltpu.roll(ft, shift, axis=1), 0.0)
        shift *= 2
    c = ft * LOG2E
    c_ref[...] = c
    ccol_ref[...] = jnp.concatenate([c, jnp.zeros((LANES - B_HEADS, SEQ), F32)], axis=0).T


def _cumsum(logf, batch):
    return pl.pallas_call(
        _cumsum_body,
        grid=(batch,),
        in_specs=[pl.BlockSpec((SEQ, LANES), lambda b: (b, 0))],
        out_specs=[pl.BlockSpec((None, B_HEADS, SEQ), lambda b: (b, 0, 0)),
                   pl.BlockSpec((SEQ, LANES), lambda b: (b, 0))],
        out_shape=[jax.ShapeDtypeStruct((batch, B_HEADS, SEQ), F32),
                   jax.ShapeDtypeStruct((batch * SEQ, LANES), F32)],
        name="forget_cumsum",
    )(logf)


P3_PITCH = BLK + 8


def _scores(q, k, bias):
    return lax.dot_general(q, k, _NT, preferred_element_type=F32) + bias


def _softmax_pv(s, v, bounded):
    m = 0.0 if bounded else jnp.max(s, axis=-1, keepdims=True)
    pe = jnp.exp2(s) if bounded else jnp.exp2(s - m)
    l = jnp.sum(pe, axis=-1, keepdims=True)
    return jnp.dot(pe.astype(BF16), v, preferred_element_type=F32), l, m


LOGIT_RANGE = 60.0
ATTN_A_LOOKAHEAD = 8


def _attn_a_body(bounded, q_ref, k_ref, v_ref, bias_ref, o_ref, nat, cm4, res2, res3):
    dil4, dil16 = DILATED_PATTERNS[1][1], DILATED_PATTERNS[2][1]
    cls_len = SEQ // dil4
    for i, ref in enumerate((q_ref, k_ref, v_ref)):
        nat[i] = ref[...].astype(F32)
        for c in range(dil4):
            cm4[i, c * cls_len:(c + 1) * cls_len, :] = nat[i, pl.ds(c, cls_len, stride=dil4), :]

    def operands(q_rows, kv_rows):
        return (cm4[0, q_rows, :].astype(BF16), cm4[1, kv_rows, :].astype(BF16), cm4[2, kv_rows, :].astype(BF16))

    def save(res, rows, acc, l, m):
        res[0, rows, :] = acc
        res[1, rows, :] = jnp.broadcast_to(l, acc.shape)
        if not bounded:
            res[2, rows, :] = jnp.broadcast_to(m, acc.shape)

    tasks = []

    def add_task(q_rows, kv_rows, bias, res, out_rows):
        def start():
            q, k, v = operands(q_rows, kv_rows)
            return _scores(q, k, bias()), v

        tasks.append((start, lambda s, v: save(res, out_rows, *_softmax_pv(s, v, bounded))))

    for c in range(dil4):
        for n in range(cls_len // BLK):
            lo = c * cls_len + n * BLK
            out_rows = pl.ds(n * BLK * dil4 + c, BLK, stride=dil4)
            if n == 0:
                add_task(slice(lo, lo + BLK), slice(lo, lo + BLK), lambda: bias_ref[1, :, BLK:], res2, out_rows)
            else:
                add_task(slice(lo, lo + BLK), slice(lo - BLK, lo + BLK), lambda: bias_ref[1], res2, out_rows)

    for c in range(dil16):
        rows = pl.ds((c % dil4) * cls_len + c // dil4, BLK, stride=dil16 // dil4)
        add_task(rows, rows, lambda: bias_ref[2, :, BLK:], res3, slice(c * P3_PITCH, c * P3_PITCH + BLK))

    def token_order(i, g):
        per_class = BLK // dil16
        return jnp.concatenate(
            [res3[i, pl.ds(g * per_class + j, dil16, stride=P3_PITCH), :] for j in range(per_class)], axis=0)

    def add_merge_task(g):
        rows = slice(g * BLK, (g + 1) * BLK)
        keys = rows if g == 0 else slice((g - 1) * BLK, (g + 1) * BLK)

        def start():
            bias = bias_ref[0, :, BLK:] if g == 0 else bias_ref[0]
            return _scores(q_ref[rows, :], k_ref[keys, :], bias), v_ref[keys, :]

        def finish(s, v):
            acc1, l1, m1 = _softmax_pv(s, v, bounded)
            if bounded:
                num = acc1 + res2[0, rows, :] + token_order(0, g)
                den = l1 + res2[1, rows, :] + token_order(1, g)
            else:
                m2, m3 = res2[2, rows, :], token_order(2, g)
                top = jnp.maximum(jnp.maximum(m1, m2), m3)
                w1, w2, w3 = jnp.exp2(m1 - top), jnp.exp2(m2 - top), jnp.exp2(m3 - top)
                num = w1 * acc1 + w2 * res2[0, rows, :] + w3 * token_order(0, g)
                den = w1 * l1 + w2 * res2[1, rows, :] + w3 * token_order(1, g)
            o_ref[rows, :] = (num * (1.0 / den)).astype(BF16)

        tasks.append((start, finish))

    for g in range(NBLK):
        add_merge_task(g)
    _emit_pipelined(tasks, ATTN_A_LOOKAHEAD)


def _attn_a(proj, bias, batch, bounded):
    n_pat = len(DILATED_PATTERNS)
    head = lambda col0: pl.BlockSpec((SEQ, HEAD_DIM), lambda b, h: (b, col0 + h))
    p3_rows = DILATED_PATTERNS[2][1] * P3_PITCH
    n_stats = 2 if bounded else 3
    return pl.pallas_call(
        functools.partial(_attn_a_body, bounded),
        grid=(batch, A_HEADS),
        in_specs=[head(COL_QA), head(COL_KA), head(COL_VA),
                  pl.BlockSpec((n_pat, None, BLK, 2 * BLK), lambda b, h: (0, h, 0, 0))],
        out_specs=pl.BlockSpec((SEQ, HEAD_DIM), lambda b, h: (b, h)),
        out_shape=jax.ShapeDtypeStruct((batch * SEQ, A_WIDTH), BF16),
        scratch_shapes=[pltpu.VMEM((3, SEQ, HEAD_DIM), F32)] * 2 + [
            pltpu.VMEM((n_stats, SEQ, HEAD_DIM), F32), pltpu.VMEM((n_stats, p3_rows, HEAD_DIM), F32)],
        compiler_params=pltpu.CompilerParams(
            dimension_semantics=("parallel", "parallel"), vmem_limit_bytes=VMEM_LIMIT),
        name="attn_dilated_bounded" if bounded else "attn_dilated",
    )(proj, proj, proj, bias)


FOX_TQ = 256


def _attn_b_body(bounded, q_ref, k_ref, v_ref, c_ref, ccol_ref, o_ref):
    h = pl.program_id(1)
    ck = c_ref[pl.ds(h, 1), :]
    qi = lax.broadcasted_iota(jnp.int32, (FOX_TQ, FOX_TQ), 0)
    kj = lax.broadcasted_iota(jnp.int32, (FOX_TQ, FOX_TQ), 1)
    causal = kj <= qi
    head_lane = lax.broadcasted_iota(jnp.int32, (FOX_TQ, LANES), 1) == h

    def task(i):
        lo, hi = i * FOX_TQ, (i + 1) * FOX_TQ

        def start():
            q = q_ref[lo:hi, :]
            s_diag = lax.dot_general(q, k_ref[lo:hi, :], _NT, preferred_element_type=F32)
            s_off = lax.dot_general(q, k_ref[:lo, :], _NT, preferred_element_type=F32) if i > 0 else None
            return s_diag, s_off

        def finish(s_diag, s_off):
            if bounded:
                cq = jnp.sum(jnp.where(head_lane, ccol_ref[lo:hi, :], 0.0), axis=-1, keepdims=True)
                p_diag = jnp.exp2(jnp.where(causal, (s_diag + cq) - ck[:, lo:hi], NEG))
                p_off = jnp.exp2((s_off + cq) - ck[:, :lo]) if i > 0 else None
            else:
                s_diag = jnp.where(causal, s_diag - ck[:, lo:hi], NEG)
                m = jnp.max(s_diag, axis=-1, keepdims=True)
                if i > 0:
                    s_off = s_off - ck[:, :lo]
                    m = jnp.maximum(m, jnp.max(s_off, axis=-1, keepdims=True))
                p_diag = jnp.exp2(s_diag - m)
                p_off = jnp.exp2(s_off - m) if i > 0 else None
            l = jnp.sum(p_diag, axis=-1, keepdims=True)
            acc = jnp.dot(p_diag.astype(BF16), v_ref[lo:hi, :], preferred_element_type=F32)
            if i > 0:
                l = l + jnp.sum(p_off, axis=-1, keepdims=True)
                acc = acc + jnp.dot(p_off.astype(BF16), v_ref[:lo, :], preferred_element_type=F32)
            o_ref[lo:hi, :] = (acc * (1.0 / l)).astype(BF16)

        return start, finish

    _emit_pipelined([task(i) for i in range(SEQ // FOX_TQ)], 2)


def _attn_b(proj, c, ccol, batch, bounded):
    head = lambda col0: pl.BlockSpec((SEQ, HEAD_DIM), lambda b, h: (b, col0 + h))
    return pl.pallas_call(
        functools.partial(_attn_b_body, bounded),
        grid=(batch, B_HEADS),
        in_specs=[head(COL_QB), head(COL_KB), head(COL_VB),
                  pl.BlockSpec((None, B_HEADS, SEQ), lambda b, h: (b, 0, 0)),
                  pl.BlockSpec((SEQ, LANES), lambda b, h: (b, 0))],
        out_specs=pl.BlockSpec((SEQ, HEAD_DIM), lambda b, h: (b, h)),
        out_shape=jax.ShapeDtypeStruct((batch * SEQ, B_WIDTH), BF16),
        compiler_params=pltpu.CompilerParams(
            dimension_semantics=("parallel", "parallel"), vmem_limit_bytes=VMEM_LIMIT),
        name="attn_forgetting_bounded" if bounded else "attn_forgetting",
    )(proj, proj, proj, c, ccol)


def _mem_kv_body(mem_ref, g_ref, w_ref, gain_ref, out_ref):
    hb = (_rms_scale(mem_ref[...]) * g_ref[...]).astype(BF16)
    acc = jnp.dot(hb, w_ref[...], preferred_element_type=F32)
    gain = gain_ref[...]
    for s in range(M_HEADS):
        sl = slice(s * M_HEAD_DIM, (s + 1) * M_HEAD_DIM)
        out_ref[:, sl] = (_rms_scale(acc[:, sl]) * gain[:, sl]).astype(BF16)
    out_ref[:, M_WIDTH:] = acc[:, M_WIDTH:].astype(BF16)


def _mem_kv(mem2, g, w, gain, batch):
    return pl.pallas_call(
        _mem_kv_body,
        grid=(batch,),
        in_specs=[
            pl.BlockSpec((MEM_LEN, D_MODEL), lambda b: (b, 0)),
            pl.BlockSpec((1, D_MODEL), lambda b: (0, 0)),
            pl.BlockSpec((D_MODEL, 2 * M_WIDTH), lambda b: (0, 0)),
            pl.BlockSpec((1, M_WIDTH), lambda b: (0, 0)),
        ],
        out_specs=pl.BlockSpec((MEM_LEN, 2 * M_WIDTH), lambda b: (b, 0)),
        out_shape=jax.ShapeDtypeStruct((batch * MEM_LEN, 2 * M_WIDTH), BF16),
        compiler_params=pltpu.CompilerParams(
            dimension_semantics=("parallel",), vmem_limit_bytes=VMEM_LIMIT),
        name="mem_kv",
    )(mem2, g, w, gain)


MEM_TQ = 512


def _attn_m_body(bounded, q_ref, k_ref, v_ref, o_ref):
    def task(i):
        rows = slice(i * MEM_TQ, (i + 1) * MEM_TQ)

        def finish(s):
            acc, l, _ = _softmax_pv(s, v_ref[...], bounded)
            o_ref[rows, :] = (acc * (1.0 / l)).astype(BF16)

        return lambda: (lax.dot_general(q_ref[rows, :], k_ref[...], _NT, preferred_element_type=F32),), finish

    _emit_pipelined([task(i) for i in range(SEQ // MEM_TQ)], 1)


def _attn_m(proj, kv, batch, bounded):
    col_qm = COL_QM * LANES // M_HEAD_DIM
    return pl.pallas_call(
        functools.partial(_attn_m_body, bounded),
        grid=(batch, M_HEADS),
        in_specs=[
            pl.BlockSpec((SEQ, M_HEAD_DIM), lambda b, h: (b, col_qm + h)),
            pl.BlockSpec((MEM_LEN, M_HEAD_DIM), lambda b, h: (b, h)),
            pl.BlockSpec((MEM_LEN, M_HEAD_DIM), lambda b, h: (b, M_HEADS + h)),
        ],
        out_specs=pl.BlockSpec((SEQ, M_HEAD_DIM), lambda b, h: (b, h)),
        out_shape=jax.ShapeDtypeStruct((batch * SEQ, M_WIDTH), BF16),
        compiler_params=pltpu.CompilerParams(
            dimension_semantics=("parallel", "parallel"), vmem_limit_bytes=VMEM_LIMIT),
        name="attn_memory_bounded" if bounded else "attn_memory",
    )(proj, kv, kv)


MERGE_TM = 256


def _merge_body(x_ref, oa_ref, za_ref, ob_ref, zb_ref, om_ref, zm_ref, gl_ref, bg_ref,
                wa_ref, wb_ref, wm_ref, wo_ref, out_ref):
    def branch(o_ref, z_ref, w_ref):
        z = z_ref[...].astype(F32)
        gated = o_ref[...].astype(F32) * (z * jax.nn.sigmoid(z))
        return jnp.dot(gated.astype(BF16), w_ref[...], preferred_element_type=F32)

    merged = None
    for n, (o_ref, z_ref, w_ref) in enumerate(((oa_ref, za_ref, wa_ref), (ob_ref, zb_ref, wb_ref),
                                                (om_ref, zm_ref, wm_ref))):
        cols = slice(n * D_MODEL, (n + 1) * D_MODEL)
        gate = jax.nn.sigmoid(gl_ref[:, cols].astype(F32) + bg_ref[:, cols])
        term = gate * branch(o_ref, z_ref, w_ref)
        merged = term if merged is None else merged + term
    out_ref[...] = x_ref[...] + jnp.dot(merged.astype(BF16), wo_ref[...], preferred_element_type=F32)


def _merge(x2, oa, ob, om, proj, bg, wa, wb, wm, wo):
    rows = x2.shape[0]
    row_blk = lambda width, colblk: pl.BlockSpec((MERGE_TM, width), lambda i: (i, colblk))
    resident = lambda shape: pl.BlockSpec(shape, lambda i: (0, 0), pipeline_mode=pl.Buffered(1))
    assert (COL_ZA * LANES) % A_WIDTH == 0 and (COL_ZB * LANES) % B_WIDTH == 0
    assert (COL_ZM * LANES) % M_WIDTH == 0 and (COL_GL * LANES) % (N_BRANCH * D_MODEL) == 0
    return pl.pallas_call(
        _merge_body,
        grid=(rows // MERGE_TM,),
        in_specs=[
            row_blk(D_MODEL, 0),
            row_blk(A_WIDTH, 0), row_blk(A_WIDTH, COL_ZA * LANES // A_WIDTH),
            row_blk(B_WIDTH, 0), row_blk(B_WIDTH, COL_ZB * LANES // B_WIDTH),
            row_blk(M_WIDTH, 0), row_blk(M_WIDTH, COL_ZM * LANES // M_WIDTH),
            row_blk(N_BRANCH * D_MODEL, COL_GL * LANES // (N_BRANCH * D_MODEL)),
            resident((1, N_BRANCH * D_MODEL)),
            resident((A_WIDTH, D_MODEL)), resident((B_WIDTH, D_MODEL)), resident((M_WIDTH, D_MODEL)),
            resident((D_MODEL, D_MODEL)),
        ],
        out_specs=row_blk(D_MODEL, 0),
        out_shape=jax.ShapeDtypeStruct((rows, D_MODEL), F32),
        compiler_params=pltpu.CompilerParams(
            dimension_semantics=("parallel",), vmem_limit_bytes=VMEM_LIMIT),
        name="merge_out",
    )(x2, oa, proj, ob, proj, om, proj, proj, bg, wa, wb, wm, wo)


def _layer(x2, mem2, batch, layer, norm_g, mem_norm_g, w_in, b_forget, b_gate, rel_bias, q_norm_a, k_norm_a,
           q_norm_b, k_norm_b, q_norm_m, k_norm_m, w_mem_kv, w_proj_a, w_proj_b, w_proj_m, w_out):
    b_f = jnp.pad(b_forget.astype(F32), (0, LANES - B_HEADS))[None]
    scale = LOG2E / math.sqrt(HEAD_DIM)
    scale_m = LOG2E / math.sqrt(M_HEAD_DIM)
    ones = lambda n: jnp.ones((n,), F32)
    gain = jnp.concatenate([
        jnp.tile(q_norm_a.astype(F32) * scale, A_HEADS), jnp.tile(k_norm_a.astype(F32), A_HEADS), ones(2 * A_WIDTH),
        jnp.tile(q_norm_b.astype(F32) * scale, B_HEADS), jnp.tile(k_norm_b.astype(F32), B_HEADS), ones(2 * B_WIDTH),
        jnp.tile(q_norm_m.astype(F32) * scale_m, M_HEADS), ones(PROJ_COLS - COL_ZM * LANES)])[None]

    w_t = jnp.swapaxes(w_in.astype(F32), 1, 2)
    h, logf, proj0 = _norm(x2, norm_g.astype(F32)[None], w_t, layer, b_f, gain)
    proj = _proj(h, w_t, layer, gain, proj0)

    def logit_bound(gq, gk, dim, scl):
        return dim * scl * jnp.max(jnp.abs(gq.astype(F32))) * jnp.max(jnp.abs(gk.astype(F32)))

    def dispatch(bound, call, *operands):
        return lax.cond(bound <= LOGIT_RANGE, functools.partial(call, bounded=True),
                        functools.partial(call, bounded=False), *operands)

    bound_a = logit_bound(q_norm_a, k_norm_a, HEAD_DIM, scale) + LOG2E * jnp.max(jnp.abs(rel_bias.astype(F32)))
    oa = dispatch(bound_a, lambda p, b, bounded: _attn_a(p, b, batch, bounded),
                  proj, _bias_tiles(rel_bias.astype(F32)))

    c, ccol = _cumsum(logf, batch)
    ob = dispatch(logit_bound(q_norm_b, k_norm_b, HEAD_DIM, scale),
                  lambda p, c_, cc, bounded: _attn_b(p, c_, cc, batch, bounded), proj, c, ccol)

    kv = _mem_kv(mem2, mem_norm_g.astype(F32)[None], w_mem_kv.astype(BF16),
                 jnp.tile(k_norm_m.astype(F32), M_HEADS)[None], batch)
    om = dispatch(logit_bound(q_norm_m, k_norm_m, M_HEAD_DIM, scale_m),
                  lambda p, kv_, bounded: _attn_m(p, kv_, batch, bounded), proj, kv)

    return _merge(x2, oa, ob, om, proj, b_gate.astype(F32).reshape(1, N_BRANCH * D_MODEL),
                  w_proj_a.astype(BF16), w_proj_b.astype(BF16), w_proj_m.astype(BF16), w_out.astype(BF16))


def kernel(x, mem, norm_g, mem_norm_g, w_in, b_forget, b_gate, rel_bias, q_norm_a, k_norm_a, q_norm_b, k_norm_b,
           q_norm_m, k_norm_m, w_mem_kv, w_proj_a, w_proj_b, w_proj_m, w_out):
    batch, seq, d_model = x.shape
    assert (seq, d_model) == (SEQ, D_MODEL) and mem.shape == (batch, MEM_LEN, D_MODEL)
    x2 = x.reshape(batch * seq, d_model)
    mem2 = mem.reshape(batch * MEM_LEN, d_model)
    depth = w_in.shape[0]
    for l in range(depth):
        x2 = _layer(x2, mem2, batch, l, norm_g[l], mem_norm_g[l], w_in, b_forget[l], b_gate[l], rel_bias,
                    q_norm_a[l], k_norm_a[l], q_norm_b[l], k_norm_b[l], q_norm_m[l], k_norm_m[l],
                    w_mem_kv[l], w_proj_a[l], w_proj_b[l], w_proj_m[l], w_out[l])
    return x2.reshape(batch, seq, d_model)
```

```python
import functools
import math

import numpy as np
import jax
import jax.numpy as jnp
from jax import lax
from jax.experimental import pallas as pl
from jax.experimental.pallas import tpu as pltpu

F32 = jnp.float32
BF16 = jnp.bfloat16

D_MODEL = 2048
SEQ = 2048
HEAD_DIM = 128
A_HEADS = 12
B_HEADS = 8
M_HEADS = 4
M_HEAD_DIM = 256
MEM_LEN = 256
A_WIDTH = A_HEADS * HEAD_DIM
B_WIDTH = B_HEADS * HEAD_DIM
M_WIDTH = M_HEADS * M_HEAD_DIM
N_BRANCH = 3
BLK = 128
NBLK = SEQ // BLK
DILATED_PATTERNS = ((128, 1), (512, 4), (2048, 16))
REL_BUCKETS = 32
REL_MAX_DIST = 2048
EPS = 1e-6
NEG = -1e30
LOG2E = math.log2(math.e)

LANES = 128
VMEM_LIMIT = 56 * 1024 * 1024

COL_QA = 0
COL_KA = COL_QA + A_HEADS
COL_VA = COL_KA + A_HEADS
COL_ZA = COL_VA + A_HEADS
COL_QB = COL_ZA + A_HEADS
COL_KB = COL_QB + B_HEADS
COL_VB = COL_KB + B_HEADS
COL_ZB = COL_VB + B_HEADS
COL_QM = COL_ZB + B_HEADS
COL_ZM = COL_QM + M_WIDTH // LANES
COL_GL = COL_ZM + M_WIDTH // LANES
COL_END = COL_GL + N_BRANCH * D_MODEL // LANES
PROJ_COLS = COL_END * LANES

NORM_TM = 1024
PROJ_TM = 2048
PROJ_TN = 1024
PROJ_CHUNK = 256
PROJ_NB_HEAD = COL_QM * LANES // PROJ_TN
F_SHIFT = B_HEADS


_NT = (((1,), (1,)), ((), ()))


def _log_sigmoid(x):
    return jnp.minimum(x, 0.0) - jnp.log1p(jnp.exp(-jnp.abs(x)))


def _rms_scale(a):
    return a * lax.rsqrt(jnp.mean(a * a, axis=-1, keepdims=True) + EPS)


def _emit_pipelined(tasks, lookahead):
    pending = []
    for start, finish in tasks:
        pending.append((finish, start()))
        if len(pending) > lookahead:
            fin, state = pending.pop(0)
            fin(*state)
    for fin, state in pending:
        fin(*state)


def _col_block_range(col0, width):
    return col0 * LANES // PROJ_TN, (col0 * LANES + width) // PROJ_TN


def _in_col_blocks(j, *ranges):
    hit = None
    for lo, hi in ranges:
        cond = (j >= lo) & (j < hi)
        hit = cond if hit is None else hit | cond
    return hit


def _head_normed(out_ref, gain_ref, hd):
    def epilogue(acc, rows):
        gain = gain_ref[...]
        for s in range(PROJ_TN // hd):
            sl = slice(s * hd, (s + 1) * hd)
            out_ref[rows, sl] = (_rms_scale(acc[:, sl]) * gain[:, sl]).astype(BF16)
    return epilogue


def _norm_body(x_ref, g_ref, wf_ref, bf_ref, wa_ref, wb_ref, h_ref, f_ref, p_ref, w16_ref):
    @pl.when(pl.program_id(0) == 0)
    def _():
        w16_ref[...] = jnp.concatenate([wa_ref[F_SHIFT:, :], wb_ref[...]], axis=0).astype(BF16)

    wf = jnp.concatenate([wf_ref[...], jnp.zeros((LANES - B_HEADS, D_MODEL), F32)], axis=0).astype(BF16)

    def epilogue(acc, rows):
        p_ref[rows, :] = acc.astype(BF16)

    tasks = []
    for r in range(NORM_TM // PROJ_CHUNK):
        rows = slice(r * PROJ_CHUNK, (r + 1) * PROJ_CHUNK)

        def start(rows=rows):
            hb = (_rms_scale(x_ref[rows, :]) * g_ref[...]).astype(BF16)
            h_ref[rows, :] = hb
            f = lax.dot_general(hb, wf, _NT, preferred_element_type=F32) + bf_ref[...]
            f_ref[rows, :] = _log_sigmoid(f)
            return (lax.dot_general(hb, w16_ref[...], _NT, preferred_element_type=F32),)

        tasks.append((start, lambda acc, rows=rows: epilogue(acc, rows)))
    _emit_pipelined(tasks, 2)


def _norm(x2, g, w_t, layer, bfp):
    rows = x2.shape[0]
    last = PROJ_COLS // PROJ_TN - 1
    return pl.pallas_call(
        _norm_body,
        grid=(rows // NORM_TM,),
        in_specs=[
            pl.BlockSpec((NORM_TM, D_MODEL), lambda i: (i, 0)),
            pl.BlockSpec((1, D_MODEL), lambda i: (0, 0)),
            pl.BlockSpec((None, B_HEADS, D_MODEL), lambda i: (layer, COL_QM * LANES // B_HEADS, 0)),
            pl.BlockSpec((1, LANES), lambda i: (0, 0)),
            pl.BlockSpec((None, PROJ_TN, D_MODEL), lambda i: (layer, last, 0), pipeline_mode=pl.Buffered(1)),
            pl.BlockSpec((None, F_SHIFT, D_MODEL), lambda i: (layer, (last + 1) * (PROJ_TN // F_SHIFT), 0)),
        ],
        out_specs=[pl.BlockSpec((NORM_TM, D_MODEL), lambda i: (i, 0)),
                   pl.BlockSpec((NORM_TM, LANES), lambda i: (i, 0)),
                   pl.BlockSpec((NORM_TM, PROJ_TN), lambda i: (i, 0))],
        out_shape=[jax.ShapeDtypeStruct((rows, D_MODEL), BF16), jax.ShapeDtypeStruct((rows, LANES), F32),
                   jax.ShapeDtypeStruct((rows, PROJ_TN), BF16)],
        scratch_shapes=[pltpu.VMEM((PROJ_TN, D_MODEL), BF16)],
        compiler_params=pltpu.CompilerParams(dimension_semantics=("arbitrary",), vmem_limit_bytes=VMEM_LIMIT),
        name="norm_proj_last",
    )(x2, g, w_t, bfp, w_t, w_t)


def _proj_body(h_ref, wa_ref, wb_ref, gain_ref, out_ref, w16_ref):
    j = pl.program_id(0)

    @pl.when((pl.program_id(1) == 0) & (j < PROJ_NB_HEAD))
    def _():
        w16_ref[...] = wa_ref[...].astype(BF16)

    @pl.when((pl.program_id(1) == 0) & (j >= PROJ_NB_HEAD))
    def _():
        w16_ref[...] = jnp.concatenate([wa_ref[F_SHIFT:, :], wb_ref[...]], axis=0).astype(BF16)

    def plain(acc, rows):
        out_ref[rows, :] = acc.astype(BF16)

    def run(epilogue):
        tasks = []
        for r in range(PROJ_TM // PROJ_CHUNK):
            rows = slice(r * PROJ_CHUNK, (r + 1) * PROJ_CHUNK)
            tasks.append((lambda rows=rows: (lax.dot_general(h_ref[rows, :], w16_ref[...], _NT,
                                                             preferred_element_type=F32),),
                          lambda acc, rows=rows: epilogue(acc, rows)))
        _emit_pipelined(tasks, 1)

    head128 = _in_col_blocks(j, _col_block_range(COL_QA, 2 * A_WIDTH), _col_block_range(COL_QB, 2 * B_WIDTH))
    head256 = _in_col_blocks(j, _col_block_range(COL_QM, M_WIDTH))
    pl.when(head128)(lambda: run(_head_normed(out_ref, gain_ref, HEAD_DIM)))
    pl.when(head256)(lambda: run(_head_normed(out_ref, gain_ref, M_HEAD_DIM)))
    pl.when(jnp.logical_not(head128 | head256))(lambda: run(plain))


def _proj(h, w_t, layer, gain):
    rows = h.shape[0]
    n_blocks = PROJ_COLS // PROJ_TN - 1

    def w_main(j, i):
        return layer, j, 0

    def w_spill(j, i):
        return layer, (jnp.maximum(j, PROJ_NB_HEAD) + 1) * (PROJ_TN // F_SHIFT), 0

    return pl.pallas_call(
        _proj_body,
        grid=(n_blocks, rows // PROJ_TM),
        in_specs=[
            pl.BlockSpec((PROJ_TM, D_MODEL), lambda j, i: (i, 0)),
            pl.BlockSpec((None, PROJ_TN, D_MODEL), w_main),
            pl.BlockSpec((None, F_SHIFT, D_MODEL), w_spill),
            pl.BlockSpec((1, PROJ_TN), lambda j, i: (0, j)),
        ],
        out_specs=pl.BlockSpec((PROJ_TM, PROJ_TN), lambda j, i: (i, j)),
        out_shape=jax.ShapeDtypeStruct((rows, n_blocks * PROJ_TN), BF16),
        scratch_shapes=[pltpu.VMEM((PROJ_TN, D_MODEL), BF16)],
        compiler_params=pltpu.CompilerParams(
            dimension_semantics=("parallel", "arbitrary"), vmem_limit_bytes=VMEM_LIMIT),
        name="proj",
    )(h, w_t, w_t, gain)


def _rel_bucket_np(dist):
    max_exact = REL_BUCKETS // 2
    d_f = np.maximum(dist, 1).astype(np.float32)
    large = max_exact + (np.log(d_f / np.float32(max_exact)) / np.float32(math.log(REL_MAX_DIST / max_exact))
                         * np.float32(REL_BUCKETS - max_exact)).astype(np.int32)
    large = np.minimum(large, REL_BUCKETS - 1)
    return np.where(dist < max_exact, dist, large).astype(np.int32)


def _bias_buckets():
    qi = np.arange(BLK)[:, None]
    kj = np.arange(2 * BLK)[None, :]
    delta = qi - kj + BLK
    tiles = []
    for window, dil in DILATED_PATTERNS:
        w_sub = window // dil
        valid = (delta >= 0) & (delta <= w_sub)
        tiles.append(np.where(valid, _rel_bucket_np(np.clip(delta, 0, w_sub) * dil), -1))
    return np.stack(tiles).astype(np.int32)


def _bias_body(tbl_ref, bucket_ref, out_ref):
    bucket = bucket_ref[...]
    for h in range(A_HEADS):
        out = jnp.full(bucket.shape, NEG, F32)
        for r in range(REL_BUCKETS):
            out = jnp.where(bucket == r, tbl_ref[r, h] * LOG2E, out)
        out_ref[h] = out


def _bias_tiles(rel_bias):
    n_pat = len(DILATED_PATTERNS)
    return pl.pallas_call(
        _bias_body,
        grid=(n_pat,),
        in_specs=[pl.BlockSpec(memory_space=pltpu.SMEM),
                  pl.BlockSpec((None, BLK, 2 * BLK), lambda p: (p, 0, 0))],
        out_specs=pl.BlockSpec((None, A_HEADS, BLK, 2 * BLK), lambda p: (p, 0, 0, 0)),
        out_shape=jax.ShapeDtypeStruct((n_pat, A_HEADS, BLK, 2 * BLK), F32),
        name="rel_bias_tiles",
    )(rel_bias, jnp.asarray(_bias_buckets()))


def _cumsum_body(f_ref, c_ref, ccol_ref):
    ft = f_ref[...].T[:B_HEADS, :]
    pos = lax.broadcasted_iota(jnp.int32, ft.shape, 1)
    shift = 1
    while shift < SEQ:
        ft = ft + jnp.where(pos >= shift, pltpu.roll(ft, shift, axis=1), 0.0)
        shift *= 2
    c = ft * LOG2E
    c_ref[...] = c
    ccol_ref[...] = jnp.concatenate([c, jnp.zeros((LANES - B_HEADS, SEQ), F32)], axis=0).T


def _cumsum(logf, batch):
    return pl.pallas_call(
        _cumsum_body,
        grid=(batch,),
        in_specs=[pl.BlockSpec((SEQ, LANES), lambda b: (b, 0))],
        out_specs=[pl.BlockSpec((None, B_HEADS, SEQ), lambda b: (b, 0, 0)),
                   pl.BlockSpec((SEQ, LANES), lambda b: (b, 0))],
        out_shape=[jax.ShapeDtypeStruct((batch, B_HEADS, SEQ), F32),
                   jax.ShapeDtypeStruct((batch * SEQ, LANES), F32)],
        name="forget_cumsum",
    )(logf)


P3_PITCH = BLK + 8


def _scores(q, k, bias):
    return lax.dot_general(q, k, _NT, preferred_element_type=F32) + bias


def _softmax_pv(s, v, bounded):
    m = 0.0 if bounded else jnp.max(s, axis=-1, keepdims=True)
    pe = jnp.exp2(s) if bounded else jnp.exp2(s - m)
    l = jnp.sum(pe, axis=-1, keepdims=True)
    return jnp.dot(pe.astype(BF16), v, preferred_element_type=F32), l, m


LOGIT_RANGE = 60.0
ATTN_A_LOOKAHEAD = 8


def _attn_a_body(bounded, q_ref, k_ref, v_ref, bias_ref, o_ref, nat, cm4, res2, res3):
    dil4, dil16 = DILATED_PATTERNS[1][1], DILATED_PATTERNS[2][1]
    cls_len = SEQ // dil4
    for i, ref in enumerate((q_ref, k_ref, v_ref)):
        nat[i] = ref[...].astype(F32)
        for c in range(dil4):
            cm4[i, c * cls_len:(c + 1) * cls_len, :] = nat[i, pl.ds(c, cls_len, stride=dil4), :]

    def operands(q_rows, kv_rows):
        return (cm4[0, q_rows, :].astype(BF16), cm4[1, kv_rows, :].astype(BF16), cm4[2, kv_rows, :].astype(BF16))

    def save(res, rows, acc, l, m):
        res[0, rows, :] = acc
        res[1, rows, :] = jnp.broadcast_to(l, acc.shape)
        if not bounded:
            res[2, rows, :] = jnp.broadcast_to(m, acc.shape)

    tasks = []

    def add_task(q_rows, kv_rows, bias, res, out_rows):
        def start():
            q, k, v = operands(q_rows, kv_rows)
            return _scores(q, k, bias()), v

        tasks.append((start, lambda s, v: save(res, out_rows, *_softmax_pv(s, v, bounded))))

    for c in range(dil4):
        for n in range(cls_len // BLK):
            lo = c * cls_len + n * BLK
            out_rows = pl.ds(n * BLK * dil4 + c, BLK, stride=dil4)
            if n == 0:
                add_task(slice(lo, lo + BLK), slice(lo, lo + BLK), lambda: bias_ref[1, :, BLK:], res2, out_rows)
            else:
                add_task(slice(lo, lo + BLK), slice(lo - BLK, lo + BLK), lambda: bias_ref[1], res2, out_rows)

    for c in range(dil16):
        rows = pl.ds((c % dil4) * cls_len + c // dil4, BLK, stride=dil16 // dil4)
        add_task(rows, rows, lambda: bias_ref[2, :, BLK:], res3, slice(c * P3_PITCH, c * P3_PITCH + BLK))

    def token_order(i, g):
        per_class = BLK // dil16
        return jnp.concatenate(
            [res3[i, pl.ds(g * per_class + j, dil16, stride=P3_PITCH), :] for j in range(per_class)], axis=0)

    def add_merge_task(g):
        rows = slice(g * BLK, (g + 1) * BLK)
        keys = rows if g == 0 else slice((g - 1) * BLK, (g + 1) * BLK)

        def start():
            bias = bias_ref[0, :, BLK:] if g == 0 else bias_ref[0]
            return _scores(q_ref[rows, :], k_ref[keys, :], bias), v_ref[keys, :]

        def finish(s, v):
            acc1, l1, m1 = _softmax_pv(s, v, bounded)
            if bounded:
                num = acc1 + res2[0, rows, :] + token_order(0, g)
                den = l1 + res2[1, rows, :] + token_order(1, g)
            else:
                m2, m3 = res2[2, rows, :], token_order(2, g)
                top = jnp.maximum(jnp.maximum(m1, m2), m3)
                w1, w2, w3 = jnp.exp2(m1 - top), jnp.exp2(m2 - top), jnp.exp2(m3 - top)
                num = w1 * acc1 + w2 * res2[0, rows, :] + w3 * token_order(0, g)
                den = w1 * l1 + w2 * res2[1, rows, :] + w3 * token_order(1, g)
            o_ref[rows, :] = (num * (1.0 / den)).astype(BF16)

        tasks.append((start, finish))

    for g in range(NBLK):
        add_merge_task(g)
    _emit_pipelined(tasks, ATTN_A_LOOKAHEAD)


def _attn_a(proj, bias, batch, bounded):
    n_pat = len(DILATED_PATTERNS)
    head = lambda col0: pl.BlockSpec((SEQ, HEAD_DIM), lambda b, h: (b, col0 + h))
    p3_rows = DILATED_PATTERNS[2][1] * P3_PITCH
    n_stats = 2 if bounded else 3
    return pl.pallas_call(
        functools.partial(_attn_a_body, bounded),
        grid=(batch, A_HEADS),
        in_specs=[head(COL_QA), head(COL_KA), head(COL_VA),
                  pl.BlockSpec((n_pat, None, BLK, 2 * BLK), lambda b, h: (0, h, 0, 0))],
        out_specs=pl.BlockSpec((SEQ, HEAD_DIM), lambda b, h: (b, h)),
        out_shape=jax.ShapeDtypeStruct((batch * SEQ, A_WIDTH), BF16),
        scratch_shapes=[pltpu.VMEM((3, SEQ, HEAD_DIM), F32)] * 2 + [
            pltpu.VMEM((n_stats, SEQ, HEAD_DIM), F32), pltpu.VMEM((n_stats, p3_rows, HEAD_DIM), F32)],
        compiler_params=pltpu.CompilerParams(
            dimension_semantics=("parallel", "parallel"), vmem_limit_bytes=VMEM_LIMIT),
        name="attn_dilated_bounded" if bounded else "attn_dilated",
    )(proj, proj, proj, bias)


FOX_TQ = 256


def _attn_b_body(bounded, q_ref, k_ref, v_ref, c_ref, ccol_ref, o_ref):
    h = pl.program_id(1)
    ck = c_ref[pl.ds(h, 1), :]
    qi = lax.broadcasted_iota(jnp.int32, (FOX_TQ, FOX_TQ), 0)
    kj = lax.broadcasted_iota(jnp.int32, (FOX_TQ, FOX_TQ), 1)
    causal = kj <= qi
    head_lane = lax.broadcasted_iota(jnp.int32, (FOX_TQ, LANES), 1) == h

    def task(i):
        lo, hi = i * FOX_TQ, (i + 1) * FOX_TQ

        def start():
            q = q_ref[lo:hi, :]
            s_diag = lax.dot_general(q, k_ref[lo:hi, :], _NT, preferred_element_type=F32)
            s_off = lax.dot_general(q, k_ref[:lo, :], _NT, preferred_element_type=F32) if i > 0 else None
            return s_diag, s_off

        def finish(s_diag, s_off):
            if bounded:
                cq = jnp.sum(jnp.where(head_lane, ccol_ref[lo:hi, :], 0.0), axis=-1, keepdims=True)
                p_diag = jnp.exp2(jnp.where(causal, (s_diag + cq) - ck[:, lo:hi], NEG))
                p_off = jnp.exp2((s_off + cq) - ck[:, :lo]) if i > 0 else None
            else:
                s_diag = jnp.where(causal, s_diag - ck[:, lo:hi], NEG)
                m = jnp.max(s_diag, axis=-1, keepdims=True)
                if i > 0:
                    s_off = s_off - ck[:, :lo]
                    m = jnp.maximum(m, jnp.max(s_off, axis=-1, keepdims=True))
                p_diag = jnp.exp2(s_diag - m)
                p_off = jnp.exp2(s_off - m) if i > 0 else None
            l = jnp.sum(p_diag, axis=-1, keepdims=True)
            acc = jnp.dot(p_diag.astype(BF16), v_ref[lo:hi, :], preferred_element_type=F32)
            if i > 0:
                l = l + jnp.sum(p_off, axis=-1, keepdims=True)
                acc = acc + jnp.dot(p_off.astype(BF16), v_ref[:lo, :], preferred_element_type=F32)
            o_ref[lo:hi, :] = (acc * (1.0 / l)).astype(BF16)

        return start, finish

    _emit_pipelined([task(i) for i in range(SEQ // FOX_TQ)], 2)


def _attn_b(proj, c, ccol, batch, bounded):
    head = lambda col0: pl.BlockSpec((SEQ, HEAD_DIM), lambda b, h: (b, col0 + h))
    return pl.pallas_call(
        functools.partial(_attn_b_body, bounded),
        grid=(batch, B_HEADS),
        in_specs=[head(COL_QB), head(COL_KB), head(COL_VB),
                  pl.BlockSpec((None, B_HEADS, SEQ), lambda b, h: (b, 0, 0)),
                  pl.BlockSpec((SEQ, LANES), lambda b, h: (b, 0))],
        out_specs=pl.BlockSpec((SEQ, HEAD_DIM), lambda b, h: (b, h)),
        out_shape=jax.ShapeDtypeStruct((batch * SEQ, B_WIDTH), BF16),
        compiler_params=pltpu.CompilerParams(
            dimension_semantics=("parallel", "parallel"), vmem_limit_bytes=VMEM_LIMIT),
        name="attn_forgetting_bounded" if bounded else "attn_forgetting",
    )(proj, proj, proj, c, ccol)


def _mem_kv_body(mem_ref, g_ref, w_ref, gain_ref, out_ref):
    hb = (_rms_scale(mem_ref[...]) * g_ref[...]).astype(BF16)
    acc = jnp.dot(hb, w_ref[...], preferred_element_type=F32)
    gain = gain_ref[...]
    for s in range(M_HEADS):
        sl = slice(s * M_HEAD_DIM, (s + 1) * M_HEAD_DIM)
        out_ref[:, sl] = (_rms_scale(acc[:, sl]) * gain[:, sl]).astype(BF16)
    out_ref[:, M_WIDTH:] = acc[:, M_WIDTH:].astype(BF16)


def _mem_kv(mem2, g, w, gain, batch):
    return pl.pallas_call(
        _mem_kv_body,
        grid=(batch,),
        in_specs=[
            pl.BlockSpec((MEM_LEN, D_MODEL), lambda b: (b, 0)),
            pl.BlockSpec((1, D_MODEL), lambda b: (0, 0)),
            pl.BlockSpec((D_MODEL, 2 * M_WIDTH), lambda b: (0, 0)),
            pl.BlockSpec((1, M_WIDTH), lambda b: (0, 0)),
        ],
        out_specs=pl.BlockSpec((MEM_LEN, 2 * M_WIDTH), lambda b: (b, 0)),
        out_shape=jax.ShapeDtypeStruct((batch * MEM_LEN, 2 * M_WIDTH), BF16),
        compiler_params=pltpu.CompilerParams(
            dimension_semantics=("parallel",), vmem_limit_bytes=VMEM_LIMIT),
        name="mem_kv",
    )(mem2, g, w, gain)


MEM_TQ = 512


def _attn_m_body(bounded, q_ref, kv_ref, o_ref):
    def task(h, i):
        rows = slice(i * MEM_TQ, (i + 1) * MEM_TQ)
        cols = slice(h * M_HEAD_DIM, (h + 1) * M_HEAD_DIM)
        v_cols = slice(M_WIDTH + h * M_HEAD_DIM, M_WIDTH + (h + 1) * M_HEAD_DIM)

        def finish(s):
            acc, l, _ = _softmax_pv(s, kv_ref[:, v_cols], bounded)
            o_ref[rows, cols] = (acc * (1.0 / l)).astype(BF16)

        return lambda: (lax.dot_general(q_ref[rows, cols], kv_ref[:, cols], _NT,
                                        preferred_element_type=F32),), finish

    _emit_pipelined([task(h, i) for h in range(M_HEADS) for i in range(SEQ // MEM_TQ)], 2)


def _attn_m(proj, kv, batch, bounded):
    assert (COL_QM * LANES) % M_WIDTH == 0
    return pl.pallas_call(
        functools.partial(_attn_m_body, bounded),
        grid=(batch,),
        in_specs=[
            pl.BlockSpec((SEQ, M_WIDTH), lambda b: (b, COL_QM * LANES // M_WIDTH)),
            pl.BlockSpec((MEM_LEN, 2 * M_WIDTH), lambda b: (b, 0)),
        ],
        out_specs=pl.BlockSpec((SEQ, M_WIDTH), lambda b: (b, 0)),
        out_shape=jax.ShapeDtypeStruct((batch * SEQ, M_WIDTH), BF16),
        compiler_params=pltpu.CompilerParams(
            dimension_semantics=("parallel",), vmem_limit_bytes=VMEM_LIMIT),
        name="attn_memory_bounded" if bounded else "attn_memory",
    )(proj, kv)


MERGE_TM = 256


def _merge_body(x_ref, oa_ref, za_ref, ob_ref, zb_ref, om_ref, zm_ref, gla_ref, glb_ref, glm_ref, glt_ref, bg_ref,
                wa_ref, wb_ref, wm_ref, wo_ref, out_ref):
    def branch(o_ref, z_ref, w_ref):
        z = z_ref[...].astype(F32)
        gated = o_ref[...].astype(F32) * (z * jax.nn.sigmoid(z))
        return jnp.dot(gated.astype(BF16), w_ref[...], preferred_element_type=F32)

    gate_logits = (lambda: gla_ref[...], lambda: glb_ref[...],
                   lambda: jnp.concatenate([glm_ref[...], glt_ref[...]], axis=1))
    merged = None
    for n, (o_ref, z_ref, w_ref) in enumerate(((oa_ref, za_ref, wa_ref), (ob_ref, zb_ref, wb_ref),
                                                (om_ref, zm_ref, wm_ref))):
        cols = slice(n * D_MODEL, (n + 1) * D_MODEL)
        gate = jax.nn.sigmoid(gate_logits[n]().astype(F32) + bg_ref[:, cols])
        term = gate * branch(o_ref, z_ref, w_ref)
        merged = term if merged is None else merged + term
    out_ref[...] = x_ref[...] + jnp.dot(merged.astype(BF16), wo_ref[...], preferred_element_type=F32)


def _merge(x2, oa, ob, om, proj, gl_tail, bg, wa, wb, wm, wo):
    rows = x2.shape[0]
    row_blk = lambda width, colblk: pl.BlockSpec((MERGE_TM, width), lambda i: (i, colblk))
    resident = lambda shape: pl.BlockSpec(shape, lambda i: (0, 0), pipeline_mode=pl.Buffered(1))
    assert (COL_ZA * LANES) % A_WIDTH == 0 and (COL_ZB * LANES) % B_WIDTH == 0
    assert (COL_ZM * LANES) % M_WIDTH == 0 and (COL_GL * LANES) % D_MODEL == 0
    assert gl_tail.shape[1] == PROJ_TN and D_MODEL % PROJ_TN == 0
    gl_col = COL_GL * LANES // D_MODEL
    gl_m_width = D_MODEL - PROJ_TN
    return pl.pallas_call(
        _merge_body,
        grid=(rows // MERGE_TM,),
        in_specs=[
            row_blk(D_MODEL, 0),
            row_blk(A_WIDTH, 0), row_blk(A_WIDTH, COL_ZA * LANES // A_WIDTH),
            row_blk(B_WIDTH, 0), row_blk(B_WIDTH, COL_ZB * LANES // B_WIDTH),
            row_blk(M_WIDTH, 0), row_blk(M_WIDTH, COL_ZM * LANES // M_WIDTH),
            row_blk(D_MODEL, gl_col), row_blk(D_MODEL, gl_col + 1),
            row_blk(gl_m_width, (gl_col + 2) * D_MODEL // gl_m_width), row_blk(PROJ_TN, 0),
            resident((1, N_BRANCH * D_MODEL)),
            resident((A_WIDTH, D_MODEL)), resident((B_WIDTH, D_MODEL)), resident((M_WIDTH, D_MODEL)),
            resident((D_MODEL, D_MODEL)),
        ],
        out_specs=row_blk(D_MODEL, 0),
        out_shape=jax.ShapeDtypeStruct((rows, D_MODEL), F32),
        compiler_params=pltpu.CompilerParams(
            dimension_semantics=("parallel",), vmem_limit_bytes=VMEM_LIMIT),
        name="merge_out",
    )(x2, oa, proj, ob, proj, om, proj, proj, proj, proj, gl_tail, bg, wa, wb, wm, wo)


def _layer(x2, mem2, batch, layer, norm_g, mem_norm_g, w_in, b_forget, b_gate, rel_bias, q_norm_a, k_norm_a,
           q_norm_b, k_norm_b, q_norm_m, k_norm_m, w_mem_kv, w_proj_a, w_proj_b, w_proj_m, w_out):
    b_f = jnp.pad(b_forget.astype(F32), (0, LANES - B_HEADS))[None]
    scale = LOG2E / math.sqrt(HEAD_DIM)
    scale_m = LOG2E / math.sqrt(M_HEAD_DIM)
    ones = lambda n: jnp.ones((n,), F32)
    gain = jnp.concatenate([
        jnp.tile(q_norm_a.astype(F32) * scale, A_HEADS), jnp.tile(k_norm_a.astype(F32), A_HEADS), ones(2 * A_WIDTH),
        jnp.tile(q_norm_b.astype(F32) * scale, B_HEADS), jnp.tile(k_norm_b.astype(F32), B_HEADS), ones(2 * B_WIDTH),
        jnp.tile(q_norm_m.astype(F32) * scale_m, M_HEADS), ones(PROJ_COLS - COL_ZM * LANES)])[None]

    w_t = jnp.swapaxes(w_in.astype(F32), 1, 2)
    h, logf, gl_tail = _norm(x2, norm_g.astype(F32)[None], w_t, layer, b_f)
    proj = _proj(h, w_t, layer, gain)

    def logit_bound(gq, gk, dim, scl):
        return dim * scl * jnp.max(jnp.abs(gq.astype(F32))) * jnp.max(jnp.abs(gk.astype(F32)))

    def dispatch(bound, call, *operands):
        return lax.cond(bound <= LOGIT_RANGE, functools.partial(call, bounded=True),
                        functools.partial(call, bounded=False), *operands)

    bound_a = logit_bound(q_norm_a, k_norm_a, HEAD_DIM, scale) + LOG2E * jnp.max(jnp.abs(rel_bias.astype(F32)))
    oa = dispatch(bound_a, lambda p, b, bounded: _attn_a(p, b, batch, bounded),
                  proj, _bias_tiles(rel_bias.astype(F32)))

    c, ccol = _cumsum(logf, batch)
    ob = dispatch(logit_bound(q_norm_b, k_norm_b, HEAD_DIM, scale),
                  lambda p, c_, cc, bounded: _attn_b(p, c_, cc, batch, bounded), proj, c, ccol)

    kv = _mem_kv(mem2, mem_norm_g.astype(F32)[None], w_mem_kv.astype(BF16),
                 jnp.tile(k_norm_m.astype(F32), M_HEADS)[None], batch)
    om = dispatch(logit_bound(q_norm_m, k_norm_m, M_HEAD_DIM, scale_m),
                  lambda p, kv_, bounded: _attn_m(p, kv_, batch, bounded), proj, kv)

    return _merge(x2, oa, ob, om, proj, gl_tail, b_gate.astype(F32).reshape(1, N_BRANCH * D_MODEL),
                  w_proj_a.astype(BF16), w_proj_b.astype(BF16), w_proj_m.astype(BF16), w_out.astype(BF16))


def kernel(x, mem, norm_g, mem_norm_g, w_in, b_forget, b_gate, rel_bias, q_norm_a, k_norm_a, q_norm_b, k_norm_b,
           q_norm_m, k_norm_m, w_mem_kv, w_proj_a, w_proj_b, w_proj_m, w_out):
    batch, seq, d_model = x.shape
    assert (seq, d_model) == (SEQ, D_MODEL) and mem.shape == (batch, MEM_LEN, D_MODEL)
    x2 = x.reshape(batch * seq, d_model)
    mem2 = mem.reshape(batch * MEM_LEN, d_model)
    depth = w_in.shape[0]
    for l in range(depth):
        x2 = _layer(x2, mem2, batch, l, norm_g[l], mem_norm_g[l], w_in, b_forget[l], b_gate[l], rel_bias,
                    q_norm_a[l], k_norm_a[l], q_norm_b[l], k_norm_b[l], q_norm_m[l], k_norm_m[l],
                    w_mem_kv[l], w_proj_a[l], w_proj_b[l], w_proj_m[l], w_out[l])
    return x2.reshape(batch, seq, d_model)
```

```python
import functools
import math

import numpy as np
import jax
import jax.numpy as jnp
from jax import lax
from jax.experimental import pallas as pl
from jax.experimental.pallas import tpu as pltpu

F32 = jnp.float32
BF16 = jnp.bfloat16

D_MODEL = 2048
SEQ = 2048
HEAD_DIM = 128
A_HEADS = 12
B_HEADS = 8
M_HEADS = 4
M_HEAD_DIM = 256
MEM_LEN = 256
A_WIDTH = A_HEADS * HEAD_DIM
B_WIDTH = B_HEADS * HEAD_DIM
M_WIDTH = M_HEADS * M_HEAD_DIM
N_BRANCH = 3
BLK = 128
NBLK = SEQ // BLK
DILATED_PATTERNS = ((128, 1), (512, 4), (2048, 16))
REL_BUCKETS = 32
REL_MAX_DIST = 2048
EPS = 1e-6
NEG = -1e30
LOG2E = math.log2(math.e)

LANES = 128
VMEM_LIMIT = 56 * 1024 * 1024

COL_QA = 0
COL_KA = COL_QA + A_HEADS
COL_VA = COL_KA + A_HEADS
COL_ZA = COL_VA + A_HEADS
COL_QB = COL_ZA + A_HEADS
COL_KB = COL_QB + B_HEADS
COL_VB = COL_KB + B_HEADS
COL_ZB = COL_VB + B_HEADS
COL_QM = COL_ZB + B_HEADS
COL_ZM = COL_QM + M_WIDTH // LANES
COL_GL = COL_ZM + M_WIDTH // LANES
COL_END = COL_GL + N_BRANCH * D_MODEL // LANES
PROJ_COLS = COL_END * LANES

NORM_TM = 1024
PROJ_TM = 2048
PROJ_TN = 1024
PROJ_CHUNK = 256
PROJ_NB_HEAD = COL_QM * LANES // PROJ_TN
F_SHIFT = B_HEADS


_NT = (((1,), (1,)), ((), ()))


def _log_sigmoid(x):
    return jnp.minimum(x, 0.0) - jnp.log1p(jnp.exp(-jnp.abs(x)))


def _rms_scale(a):
    return a * lax.rsqrt(jnp.mean(a * a, axis=-1, keepdims=True) + EPS)


def _emit_pipelined(tasks, lookahead):
    pending = []
    for start, finish in tasks:
        pending.append((finish, start()))
        if len(pending) > lookahead:
            fin, state = pending.pop(0)
            fin(*state)
    for fin, state in pending:
        fin(*state)


def _col_block_range(col0, width):
    return col0 * LANES // PROJ_TN, (col0 * LANES + width) // PROJ_TN


def _in_col_blocks(j, *ranges):
    hit = None
    for lo, hi in ranges:
        cond = (j >= lo) & (j < hi)
        hit = cond if hit is None else hit | cond
    return hit


def _head_normed(out_ref, gain_ref, hd):
    def epilogue(acc, rows):
        gain = gain_ref[...]
        for s in range(PROJ_TN // hd):
            sl = slice(s * hd, (s + 1) * hd)
            out_ref[rows, sl] = (_rms_scale(acc[:, sl]) * gain[:, sl]).astype(BF16)
    return epilogue


def _norm_body(x_ref, g_ref, wf_ref, bf_ref, wa_ref, wb_ref, h_ref, f_ref, p_ref, w16_ref):
    @pl.when(pl.program_id(0) == 0)
    def _():
        w16_ref[...] = jnp.concatenate([wa_ref[F_SHIFT:, :], wb_ref[...]], axis=0).astype(BF16)

    wf = jnp.concatenate([wf_ref[...], jnp.zeros((LANES - B_HEADS, D_MODEL), F32)], axis=0).astype(BF16)

    def epilogue(acc, rows):
        p_ref[rows, :] = acc.astype(BF16)

    tasks = []
    for r in range(NORM_TM // PROJ_CHUNK):
        rows = slice(r * PROJ_CHUNK, (r + 1) * PROJ_CHUNK)

        def start(rows=rows):
            hb = (_rms_scale(x_ref[rows, :]) * g_ref[...]).astype(BF16)
            h_ref[rows, :] = hb
            f = lax.dot_general(hb, wf, _NT, preferred_element_type=F32) + bf_ref[...]
            f_ref[rows, :] = _log_sigmoid(f)
            return (lax.dot_general(hb, w16_ref[...], _NT, preferred_element_type=F32),)

        tasks.append((start, lambda acc, rows=rows: epilogue(acc, rows)))
    _emit_pipelined(tasks, 2)


def _norm(x2, g, w_t, layer, bfp):
    rows = x2.shape[0]
    last = PROJ_COLS // PROJ_TN - 1
    return pl.pallas_call(
        _norm_body,
        grid=(rows // NORM_TM,),
        in_specs=[
            pl.BlockSpec((NORM_TM, D_MODEL), lambda i: (i, 0)),
            pl.BlockSpec((1, D_MODEL), lambda i: (0, 0)),
            pl.BlockSpec((None, B_HEADS, D_MODEL), lambda i: (layer, COL_QM * LANES // B_HEADS, 0)),
            pl.BlockSpec((1, LANES), lambda i: (0, 0)),
            pl.BlockSpec((None, PROJ_TN, D_MODEL), lambda i: (layer, last, 0), pipeline_mode=pl.Buffered(1)),
            pl.BlockSpec((None, F_SHIFT, D_MODEL), lambda i: (layer, (last + 1) * (PROJ_TN // F_SHIFT), 0)),
        ],
        out_specs=[pl.BlockSpec((NORM_TM, D_MODEL), lambda i: (i, 0)),
                   pl.BlockSpec((NORM_TM, LANES), lambda i: (i, 0)),
                   pl.BlockSpec((NORM_TM, PROJ_TN), lambda i: (i, 0))],
        out_shape=[jax.ShapeDtypeStruct((rows, D_MODEL), BF16), jax.ShapeDtypeStruct((rows, LANES), F32),
                   jax.ShapeDtypeStruct((rows, PROJ_TN), BF16)],
        scratch_shapes=[pltpu.VMEM((PROJ_TN, D_MODEL), BF16)],
        compiler_params=pltpu.CompilerParams(dimension_semantics=("arbitrary",), vmem_limit_bytes=VMEM_LIMIT),
        name="norm_proj_last",
    )(x2, g, w_t, bfp, w_t, w_t)


def _proj_body(h_ref, wa_ref, wb_ref, gain_ref, out_ref, w16_ref):
    j = pl.program_id(0)

    @pl.when((pl.program_id(1) == 0) & (j < PROJ_NB_HEAD))
    def _():
        w16_ref[...] = wa_ref[...].astype(BF16)

    @pl.when((pl.program_id(1) == 0) & (j >= PROJ_NB_HEAD))
    def _():
        w16_ref[...] = jnp.concatenate([wa_ref[F_SHIFT:, :], wb_ref[...]], axis=0).astype(BF16)

    def plain(acc, rows):
        out_ref[rows, :] = acc.astype(BF16)

    def run(epilogue):
        tasks = []
        for r in range(PROJ_TM // PROJ_CHUNK):
            rows = slice(r * PROJ_CHUNK, (r + 1) * PROJ_CHUNK)
            tasks.append((lambda rows=rows: (lax.dot_general(h_ref[rows, :], w16_ref[...], _NT,
                                                             preferred_element_type=F32),),
                          lambda acc, rows=rows: epilogue(acc, rows)))
        _emit_pipelined(tasks, 1)

    head128 = _in_col_blocks(j, _col_block_range(COL_QA, 2 * A_WIDTH), _col_block_range(COL_QB, 2 * B_WIDTH))
    head256 = _in_col_blocks(j, _col_block_range(COL_QM, M_WIDTH))
    pl.when(head128)(lambda: run(_head_normed(out_ref, gain_ref, HEAD_DIM)))
    pl.when(head256)(lambda: run(_head_normed(out_ref, gain_ref, M_HEAD_DIM)))
    pl.when(jnp.logical_not(head128 | head256))(lambda: run(plain))


def _proj(h, w_t, layer, gain):
    rows = h.shape[0]
    n_blocks = PROJ_COLS // PROJ_TN - 1

    def w_main(j, i):
        return layer, j, 0

    def w_spill(j, i):
        return layer, (jnp.maximum(j, PROJ_NB_HEAD) + 1) * (PROJ_TN // F_SHIFT), 0

    return pl.pallas_call(
        _proj_body,
        grid=(n_blocks, rows // PROJ_TM),
        in_specs=[
            pl.BlockSpec((PROJ_TM, D_MODEL), lambda j, i: (i, 0)),
            pl.BlockSpec((None, PROJ_TN, D_MODEL), w_main),
            pl.BlockSpec((None, F_SHIFT, D_MODEL), w_spill),
            pl.BlockSpec((1, PROJ_TN), lambda j, i: (0, j)),
        ],
        out_specs=pl.BlockSpec((PROJ_TM, PROJ_TN), lambda j, i: (i, j)),
        out_shape=jax.ShapeDtypeStruct((rows, n_blocks * PROJ_TN), BF16),
        scratch_shapes=[pltpu.VMEM((PROJ_TN, D_MODEL), BF16)],
        compiler_params=pltpu.CompilerParams(
            dimension_semantics=("parallel", "arbitrary"), vmem_limit_bytes=VMEM_LIMIT),
        name="proj",
    )(h, w_t, w_t, gain)


def _rel_bucket_np(dist):
    max_exact = REL_BUCKETS // 2
    d_f = np.maximum(dist, 1).astype(np.float32)
    large = max_exact + (np.log(d_f / np.float32(max_exact)) / np.float32(math.log(REL_MAX_DIST / max_exact))
                         * np.float32(REL_BUCKETS - max_exact)).astype(np.int32)
    large = np.minimum(large, REL_BUCKETS - 1)
    return np.where(dist < max_exact, dist, large).astype(np.int32)


def _bias_buckets():
    qi = np.arange(BLK)[:, None]
    kj = np.arange(2 * BLK)[None, :]
    delta = qi - kj + BLK
    tiles = []
    for window, dil in DILATED_PATTERNS:
        w_sub = window // dil
        valid = (delta >= 0) & (delta <= w_sub)
        tiles.append(np.where(valid, _rel_bucket_np(np.clip(delta, 0, w_sub) * dil), -1))
    return np.stack(tiles).astype(np.int32)


def _bias_body(tbl_ref, bucket_ref, out_ref):
    bucket = bucket_ref[...]
    for h in range(A_HEADS):
        out = jnp.full(bucket.shape, NEG, F32)
        for r in range(REL_BUCKETS):
            out = jnp.where(bucket == r, tbl_ref[r, h] * LOG2E, out)
        out_ref[h] = out


def _bias_tiles(rel_bias):
    n_pat = len(DILATED_PATTERNS)
    return pl.pallas_call(
        _bias_body,
        grid=(n_pat,),
        in_specs=[pl.BlockSpec(memory_space=pltpu.SMEM),
                  pl.BlockSpec((None, BLK, 2 * BLK), lambda p: (p, 0, 0))],
        out_specs=pl.BlockSpec((None, A_HEADS, BLK, 2 * BLK), lambda p: (p, 0, 0, 0)),
        out_shape=jax.ShapeDtypeStruct((n_pat, A_HEADS, BLK, 2 * BLK), F32),
        name="rel_bias_tiles",
    )(rel_bias, jnp.asarray(_bias_buckets()))


def _cumsum_body(f_ref, c_ref, ccol_ref):
    ft = f_ref[...].T[:B_HEADS, :]
    pos = lax.broadcasted_iota(jnp.int32, ft.shape, 1)
    shift = 1
    while shift < SEQ:
        ft = ft + jnp.where(pos >= shift, pltpu.roll(ft, shift, axis=1), 0.0)
        shift *= 2
    c = ft * LOG2E
    c_ref[...] = c
    ccol_ref[...] = jnp.concatenate([c, jnp.zeros((LANES - B_HEADS, SEQ), F32)], axis=0).T


def _cumsum(logf, batch):
    return pl.pallas_call(
        _cumsum_body,
        grid=(batch,),
        in_specs=[pl.BlockSpec((SEQ, LANES), lambda b: (b, 0))],
        out_specs=[pl.BlockSpec((None, B_HEADS, SEQ), lambda b: (b, 0, 0)),
                   pl.BlockSpec((SEQ, LANES), lambda b: (b, 0))],
        out_shape=[jax.ShapeDtypeStruct((batch, B_HEADS, SEQ), F32),
                   jax.ShapeDtypeStruct((batch * SEQ, LANES), F32)],
        name="forget_cumsum",
    )(logf)


P3_PITCH = BLK + 8


def _scores(q, k, bias):
    return lax.dot_general(q, k, _NT, preferred_element_type=F32) + bias


def _softmax_pv(s, v, bounded):
    m = 0.0 if bounded else jnp.max(s, axis=-1, keepdims=True)
    pe = jnp.exp2(s) if bounded else jnp.exp2(s - m)
    l = jnp.sum(pe, axis=-1, keepdims=True)
    return jnp.dot(pe.astype(BF16), v, preferred_element_type=F32), l, m


LOGIT_RANGE = 60.0
ATTN_A_LOOKAHEAD = {True: 4, False: 8}


def _attn_a_body(bounded, q_ref, k_ref, v_ref, bias_ref, o_ref, nat, cm4, res2, res3):
    dil4, dil16 = DILATED_PATTERNS[1][1], DILATED_PATTERNS[2][1]
    cls_len = SEQ // dil4
    for i, ref in enumerate((q_ref, k_ref, v_ref)):
        nat[i] = ref[...].astype(F32)
        for c in range(dil4):
            cm4[i, c * cls_len:(c + 1) * cls_len, :] = nat[i, pl.ds(c, cls_len, stride=dil4), :]

    def operands(q_rows, kv_rows):
        return (cm4[0, q_rows, :].astype(BF16), cm4[1, kv_rows, :].astype(BF16), cm4[2, kv_rows, :].astype(BF16))

    def save(res, rows, acc, l, m):
        res[0, rows, :] = acc
        res[1, rows, :] = jnp.broadcast_to(l, acc.shape)
        if not bounded:
            res[2, rows, :] = jnp.broadcast_to(m, acc.shape)

    tasks = []

    def add_task(q_rows, kv_rows, bias, res, out_rows):
        def start():
            q, k, v = operands(q_rows, kv_rows)
            return _scores(q, k, bias()), v

        tasks.append((start, lambda s, v: save(res, out_rows, *_softmax_pv(s, v, bounded))))

    for c in range(dil4):
        for n in range(cls_len // BLK):
            lo = c * cls_len + n * BLK
            out_rows = pl.ds(n * BLK * dil4 + c, BLK, stride=dil4)
            if n == 0:
                add_task(slice(lo, lo + BLK), slice(lo, lo + BLK), lambda: bias_ref[1, :, BLK:], res2, out_rows)
            else:
                add_task(slice(lo, lo + BLK), slice(lo - BLK, lo + BLK), lambda: bias_ref[1], res2, out_rows)

    for c in range(dil16):
        rows = pl.ds((c % dil4) * cls_len + c // dil4, BLK, stride=dil16 // dil4)
        add_task(rows, rows, lambda: bias_ref[2, :, BLK:], res3, slice(c * P3_PITCH, c * P3_PITCH + BLK))

    def token_order(i, g):
        per_class = BLK // dil16
        return jnp.concatenate(
            [res3[i, pl.ds(g * per_class + j, dil16, stride=P3_PITCH), :] for j in range(per_class)], axis=0)

    def add_merge_task(g):
        rows = slice(g * BLK, (g + 1) * BLK)
        keys = rows if g == 0 else slice((g - 1) * BLK, (g + 1) * BLK)

        def start():
            bias = bias_ref[0, :, BLK:] if g == 0 else bias_ref[0]
            return _scores(q_ref[rows, :], k_ref[keys, :], bias), v_ref[keys, :]

        def finish(s, v):
            acc1, l1, m1 = _softmax_pv(s, v, bounded)
            if bounded:
                num = acc1 + res2[0, rows, :] + token_order(0, g)
                den = l1 + res2[1, rows, :] + token_order(1, g)
            else:
                m2, m3 = res2[2, rows, :], token_order(2, g)
                top = jnp.maximum(jnp.maximum(m1, m2), m3)
                w1, w2, w3 = jnp.exp2(m1 - top), jnp.exp2(m2 - top), jnp.exp2(m3 - top)
                num = w1 * acc1 + w2 * res2[0, rows, :] + w3 * token_order(0, g)
                den = w1 * l1 + w2 * res2[1, rows, :] + w3 * token_order(1, g)
            o_ref[rows, :] = (num * (1.0 / den)).astype(BF16)

        tasks.append((start, finish))

    for g in range(NBLK):
        add_merge_task(g)
    _emit_pipelined(tasks, ATTN_A_LOOKAHEAD[bounded])


def _attn_a(proj, bias, batch, bounded):
    n_pat = len(DILATED_PATTERNS)
    head = lambda col0: pl.BlockSpec((SEQ, HEAD_DIM), lambda b, h: (b, col0 + h))
    p3_rows = DILATED_PATTERNS[2][1] * P3_PITCH
    n_stats = 2 if bounded else 3
    return pl.pallas_call(
        functools.partial(_attn_a_body, bounded),
        grid=(batch, A_HEADS),
        in_specs=[head(COL_QA), head(COL_KA), head(COL_VA),
                  pl.BlockSpec((n_pat, None, BLK, 2 * BLK), lambda b, h: (0, h, 0, 0))],
        out_specs=pl.BlockSpec((SEQ, HEAD_DIM), lambda b, h: (b, h)),
        out_shape=jax.ShapeDtypeStruct((batch * SEQ, A_WIDTH), BF16),
        scratch_shapes=[pltpu.VMEM((3, SEQ, HEAD_DIM), F32)] * 2 + [
            pltpu.VMEM((n_stats, SEQ, HEAD_DIM), F32), pltpu.VMEM((n_stats, p3_rows, HEAD_DIM), F32)],
        compiler_params=pltpu.CompilerParams(
            dimension_semantics=("parallel", "parallel"), vmem_limit_bytes=VMEM_LIMIT),
        name="attn_dilated_bounded" if bounded else "attn_dilated",
    )(proj, proj, proj, bias)


FOX_TQ = 256


def _attn_b_body(bounded, q_ref, k_ref, v_ref, c_ref, ccol_ref, o_ref):
    h = pl.program_id(1)
    ck = c_ref[pl.ds(h, 1), :]
    qi = lax.broadcasted_iota(jnp.int32, (FOX_TQ, FOX_TQ), 0)
    kj = lax.broadcasted_iota(jnp.int32, (FOX_TQ, FOX_TQ), 1)
    causal = kj <= qi
    head_lane = lax.broadcasted_iota(jnp.int32, (FOX_TQ, LANES), 1) == h

    def task(i):
        lo, hi = i * FOX_TQ, (i + 1) * FOX_TQ

        def start():
            q = q_ref[lo:hi, :]
            s_diag = lax.dot_general(q, k_ref[lo:hi, :], _NT, preferred_element_type=F32)
            s_off = lax.dot_general(q, k_ref[:lo, :], _NT, preferred_element_type=F32) if i > 0 else None
            return s_diag, s_off

        def finish(s_diag, s_off):
            if bounded:
                cq = jnp.sum(jnp.where(head_lane, ccol_ref[lo:hi, :], 0.0), axis=-1, keepdims=True)
                p_diag = jnp.exp2(jnp.where(causal, (s_diag + cq) - ck[:, lo:hi], NEG))
                p_off = jnp.exp2((s_off + cq) - ck[:, :lo]) if i > 0 else None
            else:
                s_diag = jnp.where(causal, s_diag - ck[:, lo:hi], NEG)
                m = jnp.max(s_diag, axis=-1, keepdims=True)
                if i > 0:
                    s_off = s_off - ck[:, :lo]
                    m = jnp.maximum(m, jnp.max(s_off, axis=-1, keepdims=True))
                p_diag = jnp.exp2(s_diag - m)
                p_off = jnp.exp2(s_off - m) if i > 0 else None
            l = jnp.sum(p_diag, axis=-1, keepdims=True)
            acc = jnp.dot(p_diag.astype(BF16), v_ref[lo:hi, :], preferred_element_type=F32)
            if i > 0:
                l = l + jnp.sum(p_off, axis=-1, keepdims=True)
                acc = acc + jnp.dot(p_off.astype(BF16), v_ref[:lo, :], preferred_element_type=F32)
            o_ref[lo:hi, :] = (acc * (1.0 / l)).astype(BF16)

        return start, finish

    _emit_pipelined([task(i) for i in range(SEQ // FOX_TQ)], 2)


def _attn_b(proj, c, ccol, batch, bounded):
    head = lambda col0: pl.BlockSpec((SEQ, HEAD_DIM), lambda b, h: (b, col0 + h))
    return pl.pallas_call(
        functools.partial(_attn_b_body, bounded),
        grid=(batch, B_HEADS),
        in_specs=[head(COL_QB), head(COL_KB), head(COL_VB),
                  pl.BlockSpec((None, B_HEADS, SEQ), lambda b, h: (b, 0, 0)),
                  pl.BlockSpec((SEQ, LANES), lambda b, h: (b, 0))],
        out_specs=pl.BlockSpec((SEQ, HEAD_DIM), lambda b, h: (b, h)),
        out_shape=jax.ShapeDtypeStruct((batch * SEQ, B_WIDTH), BF16),
        compiler_params=pltpu.CompilerParams(
            dimension_semantics=("parallel", "parallel"), vmem_limit_bytes=VMEM_LIMIT),
        name="attn_forgetting_bounded" if bounded else "attn_forgetting",
    )(proj, proj, proj, c, ccol)


def _mem_kv_body(mem_ref, g_ref, w_ref, gain_ref, out_ref):
    hb = (_rms_scale(mem_ref[...]) * g_ref[...]).astype(BF16)
    acc = jnp.dot(hb, w_ref[...], preferred_element_type=F32)
    gain = gain_ref[...]
    for s in range(M_HEADS):
        sl = slice(s * M_HEAD_DIM, (s + 1) * M_HEAD_DIM)
        out_ref[:, sl] = (_rms_scale(acc[:, sl]) * gain[:, sl]).astype(BF16)
    out_ref[:, M_WIDTH:] = acc[:, M_WIDTH:].astype(BF16)


def _mem_kv(mem2, g, w, gain, batch):
    return pl.pallas_call(
        _mem_kv_body,
        grid=(batch,),
        in_specs=[
            pl.BlockSpec((MEM_LEN, D_MODEL), lambda b: (b, 0)),
            pl.BlockSpec((1, D_MODEL), lambda b: (0, 0)),
            pl.BlockSpec((D_MODEL, 2 * M_WIDTH), lambda b: (0, 0)),
            pl.BlockSpec((1, M_WIDTH), lambda b: (0, 0)),
        ],
        out_specs=pl.BlockSpec((MEM_LEN, 2 * M_WIDTH), lambda b: (b, 0)),
        out_shape=jax.ShapeDtypeStruct((batch * MEM_LEN, 2 * M_WIDTH), BF16),
        compiler_params=pltpu.CompilerParams(
            dimension_semantics=("parallel",), vmem_limit_bytes=VMEM_LIMIT),
        name="mem_kv",
    )(mem2, g, w, gain)


MEM_TQ = 512


def _attn_m_body(bounded, q_ref, kv_ref, o_ref):
    def task(h, i):
        rows = slice(i * MEM_TQ, (i + 1) * MEM_TQ)
        cols = slice(h * M_HEAD_DIM, (h + 1) * M_HEAD_DIM)
        v_cols = slice(M_WIDTH + h * M_HEAD_DIM, M_WIDTH + (h + 1) * M_HEAD_DIM)

        def finish(s):
            acc, l, _ = _softmax_pv(s, kv_ref[:, v_cols], bounded)
            o_ref[rows, cols] = (acc * (1.0 / l)).astype(BF16)

        return lambda: (lax.dot_general(q_ref[rows, cols], kv_ref[:, cols], _NT,
                                        preferred_element_type=F32),), finish

    _emit_pipelined([task(h, i) for h in range(M_HEADS) for i in range(SEQ // MEM_TQ)], 2)


def _attn_m(proj, kv, batch, bounded):
    assert (COL_QM * LANES) % M_WIDTH == 0
    return pl.pallas_call(
        functools.partial(_attn_m_body, bounded),
        grid=(batch,),
        in_specs=[
            pl.BlockSpec((SEQ, M_WIDTH), lambda b: (b, COL_QM * LANES // M_WIDTH)),
            pl.BlockSpec((MEM_LEN, 2 * M_WIDTH), lambda b: (b, 0)),
        ],
        out_specs=pl.BlockSpec((SEQ, M_WIDTH), lambda b: (b, 0)),
        out_shape=jax.ShapeDtypeStruct((batch * SEQ, M_WIDTH), BF16),
        compiler_params=pltpu.CompilerParams(
            dimension_semantics=("parallel",), vmem_limit_bytes=VMEM_LIMIT),
        name="attn_memory_bounded" if bounded else "attn_memory",
    )(proj, kv)


MERGE_TM = 256


def _merge_body(x_ref, oa_ref, za_ref, ob_ref, zb_ref, om_ref, zm_ref, gla_ref, glb_ref, glm_ref, glt_ref, bg_ref,
                wa_ref, wb_ref, wm_ref, wo_ref, out_ref):
    def branch(o_ref, z_ref, w_ref):
        z = z_ref[...].astype(F32)
        gated = o_ref[...].astype(F32) * (z * jax.nn.sigmoid(z))
        return jnp.dot(gated.astype(BF16), w_ref[...], preferred_element_type=F32)

    gate_logits = (lambda: gla_ref[...], lambda: glb_ref[...],
                   lambda: jnp.concatenate([glm_ref[...], glt_ref[...]], axis=1))
    merged = None
    for n, (o_ref, z_ref, w_ref) in enumerate(((oa_ref, za_ref, wa_ref), (ob_ref, zb_ref, wb_ref),
                                                (om_ref, zm_ref, wm_ref))):
        cols = slice(n * D_MODEL, (n + 1) * D_MODEL)
        gate = jax.nn.sigmoid(gate_logits[n]().astype(F32) + bg_ref[:, cols])
        term = gate * branch(o_ref, z_ref, w_ref)
        merged = term if merged is None else merged + term
    out_ref[...] = x_ref[...] + jnp.dot(merged.astype(BF16), wo_ref[...], preferred_element_type=F32)


def _merge(x2, oa, ob, om, proj, gl_tail, bg, wa, wb, wm, wo):
    rows = x2.shape[0]
    row_blk = lambda width, colblk: pl.BlockSpec((MERGE_TM, width), lambda i: (i, colblk))
    resident = lambda shape: pl.BlockSpec(shape, lambda i: (0, 0), pipeline_mode=pl.Buffered(1))
    assert (COL_ZA * LANES) % A_WIDTH == 0 and (COL_ZB * LANES) % B_WIDTH == 0
    assert (COL_ZM * LANES) % M_WIDTH == 0 and (COL_GL * LANES) % D_MODEL == 0
    assert gl_tail.shape[1] == PROJ_TN and D_MODEL % PROJ_TN == 0
    gl_col = COL_GL * LANES // D_MODEL
    gl_m_width = D_MODEL - PROJ_TN
    return pl.pallas_call(
        _merge_body,
        grid=(rows // MERGE_TM,),
        in_specs=[
            row_blk(D_MODEL, 0),
            row_blk(A_WIDTH, 0), row_blk(A_WIDTH, COL_ZA * LANES // A_WIDTH),
            row_blk(B_WIDTH, 0), row_blk(B_WIDTH, COL_ZB * LANES // B_WIDTH),
            row_blk(M_WIDTH, 0), row_blk(M_WIDTH, COL_ZM * LANES // M_WIDTH),
            row_blk(D_MODEL, gl_col), row_blk(D_MODEL, gl_col + 1),
            row_blk(gl_m_width, (gl_col + 2) * D_MODEL // gl_m_width), row_blk(PROJ_TN, 0),
            resident((1, N_BRANCH * D_MODEL)),
            resident((A_WIDTH, D_MODEL)), resident((B_WIDTH, D_MODEL)), resident((M_WIDTH, D_MODEL)),
            resident((D_MODEL, D_MODEL)),
        ],
        out_specs=row_blk(D_MODEL, 0),
        out_shape=jax.ShapeDtypeStruct((rows, D_MODEL), F32),
        compiler_params=pltpu.CompilerParams(
            dimension_semantics=("parallel",), vmem_limit_bytes=VMEM_LIMIT),
        name="merge_out",
    )(x2, oa, proj, ob, proj, om, proj, proj, proj, proj, gl_tail, bg, wa, wb, wm, wo)


def _layer(x2, mem2, batch, layer, norm_g, mem_norm_g, w_in, b_forget, b_gate, rel_bias, q_norm_a, k_norm_a,
           q_norm_b, k_norm_b, q_norm_m, k_norm_m, w_mem_kv, w_proj_a, w_proj_b, w_proj_m, w_out):
    b_f = jnp.pad(b_forget.astype(F32), (0, LANES - B_HEADS))[None]
    scale = LOG2E / math.sqrt(HEAD_DIM)
    scale_m = LOG2E / math.sqrt(M_HEAD_DIM)
    ones = lambda n: jnp.ones((n,), F32)
    gain = jnp.concatenate([
        jnp.tile(q_norm_a.astype(F32) * scale, A_HEADS), jnp.tile(k_norm_a.astype(F32), A_HEADS), ones(2 * A_WIDTH),
        jnp.tile(q_norm_b.astype(F32) * scale, B_HEADS), jnp.tile(k_norm_b.astype(F32), B_HEADS), ones(2 * B_WIDTH),
        jnp.tile(q_norm_m.astype(F32) * scale_m, M_HEADS), ones(PROJ_COLS - COL_ZM * LANES)])[None]

    w_t = jnp.swapaxes(w_in.astype(F32), 1, 2)
    h, logf, gl_tail = _norm(x2, norm_g.astype(F32)[None], w_t, layer, b_f)
    proj = _proj(h, w_t, layer, gain)

    def logit_bound(gq, gk, dim, scl):
        return dim * scl * jnp.max(jnp.abs(gq.astype(F32))) * jnp.max(jnp.abs(gk.astype(F32)))

    def dispatch(bound, call, *operands):
        return lax.cond(bound <= LOGIT_RANGE, functools.partial(call, bounded=True),
                        functools.partial(call, bounded=False), *operands)

    bound_a = logit_bound(q_norm_a, k_norm_a, HEAD_DIM, scale) + LOG2E * jnp.max(jnp.abs(rel_bias.astype(F32)))
    oa = dispatch(bound_a, lambda p, b, bounded: _attn_a(p, b, batch, bounded),
                  proj, _bias_tiles(rel_bias.astype(F32)))

    c, ccol = _cumsum(logf, batch)
    ob = dispatch(logit_bound(q_norm_b, k_norm_b, HEAD_DIM, scale),
                  lambda p, c_, cc, bounded: _attn_b(p, c_, cc, batch, bounded), proj, c, ccol)

    kv = _mem_kv(mem2, mem_norm_g.astype(F32)[None], w_mem_kv.astype(BF16),
                 jnp.tile(k_norm_m.astype(F32), M_HEADS)[None], batch)
    om = dispatch(logit_bound(q_norm_m, k_norm_m, M_HEAD_DIM, scale_m),
                  lambda p, kv_, bounded: _attn_m(p, kv_, batch, bounded), proj, kv)

    return _merge(x2, oa, ob, om, proj, gl_tail, b_gate.astype(F32).reshape(1, N_BRANCH * D_MODEL),
                  w_proj_a.astype(BF16), w_proj_b.astype(BF16), w_proj_m.astype(BF16), w_out.astype(BF16))


def kernel(x, mem, norm_g, mem_norm_g, w_in, b_forget, b_gate, rel_bias, q_norm_a, k_norm_a, q_norm_b, k_norm_b,
           q_norm_m, k_norm_m, w_mem_kv, w_proj_a, w_proj_b, w_proj_m, w_out):
    batch, seq, d_model = x.shape
    assert (seq, d_model) == (SEQ, D_MODEL) and mem.shape == (batch, MEM_LEN, D_MODEL)
    x2 = x.reshape(batch * seq, d_model)
    mem2 = mem.reshape(batch * MEM_LEN, d_model)
    depth = w_in.shape[0]
    for l in range(depth):
        x2 = _layer(x2, mem2, batch, l, norm_g[l], mem_norm_g[l], w_in, b_forget[l], b_gate[l], rel_bias,
                    q_norm_a[l], k_norm_a[l], q_norm_b[l], k_norm_b[l], q_norm_m[l], k_norm_m[l],
                    w_mem_kv[l], w_proj_a[l], w_proj_b[l], w_proj_m[l], w_out[l])
    return x2.reshape(batch, seq, d_model)
```

```python
import functools
import math

import numpy as np
import jax
import jax.numpy as jnp
from jax import lax
from jax.experimental import pallas as pl
from jax.experimental.pallas import tpu as pltpu

F32 = jnp.float32
BF16 = jnp.bfloat16

D_MODEL = 2048
SEQ = 2048
HEAD_DIM = 128
A_HEADS = 12
B_HEADS = 8
M_HEADS = 4
M_HEAD_DIM = 256
MEM_LEN = 256
A_WIDTH = A_HEADS * HEAD_DIM
B_WIDTH = B_HEADS * HEAD_DIM
M_WIDTH = M_HEADS * M_HEAD_DIM
N_BRANCH = 3
BLK = 128
NBLK = SEQ // BLK
DILATED_PATTERNS = ((128, 1), (512, 4), (2048, 16))
REL_BUCKETS = 32
REL_MAX_DIST = 2048
EPS = 1e-6
NEG = -1e30
LOG2E = math.log2(math.e)

LANES = 128
VMEM_LIMIT = 56 * 1024 * 1024

COL_QA = 0
COL_KA = COL_QA + A_HEADS
COL_VA = COL_KA + A_HEADS
COL_ZA = COL_VA + A_HEADS
COL_QB = COL_ZA + A_HEADS
COL_KB = COL_QB + B_HEADS
COL_VB = COL_KB + B_HEADS
COL_ZB = COL_VB + B_HEADS
COL_QM = COL_ZB + B_HEADS
COL_ZM = COL_QM + M_WIDTH // LANES
COL_GL = COL_ZM + M_WIDTH // LANES
COL_END = COL_GL + N_BRANCH * D_MODEL // LANES
PROJ_COLS = COL_END * LANES

NORM_TM = 1024
PROJ_TM = 2048
PROJ_TN = 1024
PROJ_CHUNK = 256
PROJ_NB_HEAD = COL_QM * LANES // PROJ_TN
F_SHIFT = B_HEADS


_NT = (((1,), (1,)), ((), ()))


def _log_sigmoid(x):
    return jnp.minimum(x, 0.0) - jnp.log1p(jnp.exp(-jnp.abs(x)))


def _rms_scale(a):
    return a * lax.rsqrt(jnp.mean(a * a, axis=-1, keepdims=True) + EPS)


def _emit_pipelined(tasks, lookahead):
    pending = []
    for start, finish in tasks:
        pending.append((finish, start()))
        if len(pending) > lookahead:
            fin, state = pending.pop(0)
            fin(*state)
    for fin, state in pending:
        fin(*state)


def _col_block_range(col0, width):
    return col0 * LANES // PROJ_TN, (col0 * LANES + width) // PROJ_TN


def _in_col_blocks(j, *ranges):
    hit = None
    for lo, hi in ranges:
        cond = (j >= lo) & (j < hi)
        hit = cond if hit is None else hit | cond
    return hit


def _head_normed(out_ref, gain_ref, hd):
    def epilogue(acc, rows):
        gain = gain_ref[...]
        for s in range(PROJ_TN // hd):
            sl = slice(s * hd, (s + 1) * hd)
            out_ref[rows, sl] = (_rms_scale(acc[:, sl]) * gain[:, sl]).astype(BF16)
    return epilogue


def _norm_body(x_ref, g_ref, wf_ref, bf_ref, wa_ref, wb_ref, h_ref, f_ref, p_ref, w16_ref):
    @pl.when(pl.program_id(0) == 0)
    def _():
        w16_ref[...] = jnp.concatenate([wa_ref[F_SHIFT:, :], wb_ref[...]], axis=0).astype(BF16)

    wf = jnp.concatenate([wf_ref[...], jnp.zeros((LANES - B_HEADS, D_MODEL), F32)], axis=0).astype(BF16)

    def epilogue(acc, rows):
        p_ref[rows, :] = acc.astype(BF16)

    tasks = []
    for r in range(NORM_TM // PROJ_CHUNK):
        rows = slice(r * PROJ_CHUNK, (r + 1) * PROJ_CHUNK)

        def start(rows=rows):
            hb = (_rms_scale(x_ref[rows, :]) * g_ref[...]).astype(BF16)
            h_ref[rows, :] = hb
            f = lax.dot_general(hb, wf, _NT, preferred_element_type=F32) + bf_ref[...]
            f_ref[rows, :] = _log_sigmoid(f)
            return (lax.dot_general(hb, w16_ref[...], _NT, preferred_element_type=F32),)

        tasks.append((start, lambda acc, rows=rows: epilogue(acc, rows)))
    _emit_pipelined(tasks, 2)


def _norm(x2, g, w_t, layer, bfp):
    rows = x2.shape[0]
    last = PROJ_COLS // PROJ_TN - 1
    return pl.pallas_call(
        _norm_body,
        grid=(rows // NORM_TM,),
        in_specs=[
            pl.BlockSpec((NORM_TM, D_MODEL), lambda i: (i, 0)),
            pl.BlockSpec((1, D_MODEL), lambda i: (0, 0)),
            pl.BlockSpec((None, B_HEADS, D_MODEL), lambda i: (layer, COL_QM * LANES // B_HEADS, 0)),
            pl.BlockSpec((1, LANES), lambda i: (0, 0)),
            pl.BlockSpec((None, PROJ_TN, D_MODEL), lambda i: (layer, last, 0), pipeline_mode=pl.Buffered(1)),
            pl.BlockSpec((None, F_SHIFT, D_MODEL), lambda i: (layer, (last + 1) * (PROJ_TN // F_SHIFT), 0)),
        ],
        out_specs=[pl.BlockSpec((NORM_TM, D_MODEL), lambda i: (i, 0)),
                   pl.BlockSpec((NORM_TM, LANES), lambda i: (i, 0)),
                   pl.BlockSpec((NORM_TM, PROJ_TN), lambda i: (i, 0))],
        out_shape=[jax.ShapeDtypeStruct((rows, D_MODEL), BF16), jax.ShapeDtypeStruct((rows, LANES), F32),
                   jax.ShapeDtypeStruct((rows, PROJ_TN), BF16)],
        scratch_shapes=[pltpu.VMEM((PROJ_TN, D_MODEL), BF16)],
        compiler_params=pltpu.CompilerParams(dimension_semantics=("arbitrary",), vmem_limit_bytes=VMEM_LIMIT),
        name="norm_proj_last",
    )(x2, g, w_t, bfp, w_t, w_t)


def _proj_body(h_ref, wa_ref, wb_ref, gain_ref, out_ref, w16_ref):
    j = pl.program_id(0)

    @pl.when((pl.program_id(1) == 0) & (j < PROJ_NB_HEAD))
    def _():
        w16_ref[...] = wa_ref[...].astype(BF16)

    @pl.when((pl.program_id(1) == 0) & (j >= PROJ_NB_HEAD))
    def _():
        w16_ref[...] = jnp.concatenate([wa_ref[F_SHIFT:, :], wb_ref[...]], axis=0).astype(BF16)

    def plain(acc, rows):
        out_ref[rows, :] = acc.astype(BF16)

    def run(epilogue):
        tasks = []
        for r in range(PROJ_TM // PROJ_CHUNK):
            rows = slice(r * PROJ_CHUNK, (r + 1) * PROJ_CHUNK)
            tasks.append((lambda rows=rows: (lax.dot_general(h_ref[rows, :], w16_ref[...], _NT,
                                                             preferred_element_type=F32),),
                          lambda acc, rows=rows: epilogue(acc, rows)))
        _emit_pipelined(tasks, 1)

    head128 = _in_col_blocks(j, _col_block_range(COL_QA, 2 * A_WIDTH), _col_block_range(COL_QB, 2 * B_WIDTH))
    head256 = _in_col_blocks(j, _col_block_range(COL_QM, M_WIDTH))
    pl.when(head128)(lambda: run(_head_normed(out_ref, gain_ref, HEAD_DIM)))
    pl.when(head256)(lambda: run(_head_normed(out_ref, gain_ref, M_HEAD_DIM)))
    pl.when(jnp.logical_not(head128 | head256))(lambda: run(plain))


def _proj(h, w_t, layer, gain):
    rows = h.shape[0]
    n_blocks = PROJ_COLS // PROJ_TN - 1

    def w_main(j, i):
        return layer, j, 0

    def w_spill(j, i):
        return layer, (jnp.maximum(j, PROJ_NB_HEAD) + 1) * (PROJ_TN // F_SHIFT), 0

    return pl.pallas_call(
        _proj_body,
        grid=(n_blocks, rows // PROJ_TM),
        in_specs=[
            pl.BlockSpec((PROJ_TM, D_MODEL), lambda j, i: (i, 0)),
            pl.BlockSpec((None, PROJ_TN, D_MODEL), w_main),
            pl.BlockSpec((None, F_SHIFT, D_MODEL), w_spill),
            pl.BlockSpec((1, PROJ_TN), lambda j, i: (0, j)),
        ],
        out_specs=pl.BlockSpec((PROJ_TM, PROJ_TN), lambda j, i: (i, j)),
        out_shape=jax.ShapeDtypeStruct((rows, n_blocks * PROJ_TN), BF16),
        scratch_shapes=[pltpu.VMEM((PROJ_TN, D_MODEL), BF16)],
        compiler_params=pltpu.CompilerParams(
            dimension_semantics=("parallel", "arbitrary"), vmem_limit_bytes=VMEM_LIMIT),
        name="proj",
    )(h, w_t, w_t, gain)


def _rel_bucket_np(dist):
    max_exact = REL_BUCKETS // 2
    d_f = np.maximum(dist, 1).astype(np.float32)
    large = max_exact + (np.log(d_f / np.float32(max_exact)) / np.float32(math.log(REL_MAX_DIST / max_exact))
                         * np.float32(REL_BUCKETS - max_exact)).astype(np.int32)
    large = np.minimum(large, REL_BUCKETS - 1)
    return np.where(dist < max_exact, dist, large).astype(np.int32)


def _bias_buckets():
    qi = np.arange(BLK)[:, None]
    kj = np.arange(2 * BLK)[None, :]
    delta = qi - kj + BLK
    tiles = []
    for window, dil in DILATED_PATTERNS:
        w_sub = window // dil
        valid = (delta >= 0) & (delta <= w_sub)
        tiles.append(np.where(valid, _rel_bucket_np(np.clip(delta, 0, w_sub) * dil), -1))
    return np.stack(tiles).astype(np.int32)


def _bias_body(tbl_ref, bucket_ref, out_ref):
    bucket = bucket_ref[...]
    for h in range(A_HEADS):
        out = jnp.full(bucket.shape, NEG, F32)
        for r in range(REL_BUCKETS):
            out = jnp.where(bucket == r, tbl_ref[r, h] * LOG2E, out)
        out_ref[h] = out


def _bias_tiles(rel_bias):
    n_pat = len(DILATED_PATTERNS)
    return pl.pallas_call(
        _bias_body,
        grid=(n_pat,),
        in_specs=[pl.BlockSpec(memory_space=pltpu.SMEM),
                  pl.BlockSpec((None, BLK, 2 * BLK), lambda p: (p, 0, 0))],
        out_specs=pl.BlockSpec((None, A_HEADS, BLK, 2 * BLK), lambda p: (p, 0, 0, 0)),
        out_shape=jax.ShapeDtypeStruct((n_pat, A_HEADS, BLK, 2 * BLK), F32),
        name="rel_bias_tiles",
    )(rel_bias, jnp.asarray(_bias_buckets()))


def _cumsum_body(f_ref, c_ref, ccol_ref):
    ft = f_ref[...].T[:B_HEADS, :]
    pos = lax.broadcasted_iota(jnp.int32, ft.shape, 1)
    shift = 1
    while shift < SEQ:
        ft = ft + jnp.where(pos >= shift, pltpu.roll(ft, shift, axis=1), 0.0)
        shift *= 2
    c = ft * LOG2E
    c_ref[...] = c
    ccol_ref[...] = jnp.concatenate([c, jnp.zeros((LANES - B_HEADS, SEQ), F32)], axis=0).T


def _cumsum(logf, batch):
    return pl.pallas_call(
        _cumsum_body,
        grid=(batch,),
        in_specs=[pl.BlockSpec((SEQ, LANES), lambda b: (b, 0))],
        out_specs=[pl.BlockSpec((None, B_HEADS, SEQ), lambda b: (b, 0, 0)),
                   pl.BlockSpec((SEQ, LANES), lambda b: (b, 0))],
        out_shape=[jax.ShapeDtypeStruct((batch, B_HEADS, SEQ), F32),
                   jax.ShapeDtypeStruct((batch * SEQ, LANES), F32)],
        name="forget_cumsum",
    )(logf)


P3_PITCH = BLK + 8


def _scores(q, k, bias):
    return lax.dot_general(q, k, _NT, preferred_element_type=F32) + bias


def _softmax_pv(s, v, bounded):
    m = 0.0 if bounded else jnp.max(s, axis=-1, keepdims=True)
    pe = jnp.exp2(s) if bounded else jnp.exp2(s - m)
    l = jnp.sum(pe, axis=-1, keepdims=True)
    return jnp.dot(pe.astype(BF16), v, preferred_element_type=F32), l, m


LOGIT_RANGE = 60.0
ATTN_A_LOOKAHEAD = {True: 4, False: 8}


def _attn_a_body(bounded, q_ref, k_ref, v_ref, bias_ref, o_ref, nat, cm4, res2, res3):
    dil4, dil16 = DILATED_PATTERNS[1][1], DILATED_PATTERNS[2][1]
    cls_len = SEQ // dil4
    for i, ref in enumerate((q_ref, k_ref, v_ref)):
        nat[i] = ref[...].astype(F32)
        for c in range(dil4):
            cm4[i, c * cls_len:(c + 1) * cls_len, :] = nat[i, pl.ds(c, cls_len, stride=dil4), :]

    def operands(q_rows, kv_rows):
        return (cm4[0, q_rows, :].astype(BF16), cm4[1, kv_rows, :].astype(BF16), cm4[2, kv_rows, :].astype(BF16))

    def save(res, rows, acc, l, m):
        res[0, rows, :] = acc
        res[1, rows, :] = jnp.broadcast_to(l, acc.shape)
        if not bounded:
            res[2, rows, :] = jnp.broadcast_to(m, acc.shape)

    tasks = []

    def add_task(q_rows, kv_rows, bias, res, out_rows):
        def start():
            q, k, v = operands(q_rows, kv_rows)
            return _scores(q, k, bias()), v

        tasks.append((start, lambda s, v: save(res, out_rows, *_softmax_pv(s, v, bounded))))

    for c in range(dil4):
        for n in range(cls_len // BLK):
            lo = c * cls_len + n * BLK
            out_rows = pl.ds(n * BLK * dil4 + c, BLK, stride=dil4)
            if n == 0:
                add_task(slice(lo, lo + BLK), slice(lo, lo + BLK), lambda: bias_ref[1, :, BLK:], res2, out_rows)
            else:
                add_task(slice(lo, lo + BLK), slice(lo - BLK, lo + BLK), lambda: bias_ref[1], res2, out_rows)

    for c in range(dil16):
        rows = pl.ds((c % dil4) * cls_len + c // dil4, BLK, stride=dil16 // dil4)
        add_task(rows, rows, lambda: bias_ref[2, :, BLK:], res3, slice(c * P3_PITCH, c * P3_PITCH + BLK))

    def token_order(i, g):
        per_class = BLK // dil16
        return jnp.concatenate(
            [res3[i, pl.ds(g * per_class + j, dil16, stride=P3_PITCH), :] for j in range(per_class)], axis=0)

    def add_merge_task(g):
        rows = slice(g * BLK, (g + 1) * BLK)
        keys = rows if g == 0 else slice((g - 1) * BLK, (g + 1) * BLK)

        def start():
            bias = bias_ref[0, :, BLK:] if g == 0 else bias_ref[0]
            return _scores(q_ref[rows, :], k_ref[keys, :], bias), v_ref[keys, :]

        def finish(s, v):
            acc1, l1, m1 = _softmax_pv(s, v, bounded)
            if bounded:
                num = acc1 + res2[0, rows, :] + token_order(0, g)
                den = l1 + res2[1, rows, :] + token_order(1, g)
            else:
                m2, m3 = res2[2, rows, :], token_order(2, g)
                top = jnp.maximum(jnp.maximum(m1, m2), m3)
                w1, w2, w3 = jnp.exp2(m1 - top), jnp.exp2(m2 - top), jnp.exp2(m3 - top)
                num = w1 * acc1 + w2 * res2[0, rows, :] + w3 * token_order(0, g)
                den = w1 * l1 + w2 * res2[1, rows, :] + w3 * token_order(1, g)
            o_ref[rows, :] = (num * (1.0 / den)).astype(BF16)

        tasks.append((start, finish))

    for g in range(NBLK):
        add_merge_task(g)
    _emit_pipelined(tasks, ATTN_A_LOOKAHEAD[bounded])


def _attn_a(proj, bias, batch, bounded):
    n_pat = len(DILATED_PATTERNS)
    head = lambda col0: pl.BlockSpec((SEQ, HEAD_DIM), lambda b, h: (b, col0 + h))
    p3_rows = DILATED_PATTERNS[2][1] * P3_PITCH
    n_stats = 2 if bounded else 3
    return pl.pallas_call(
        functools.partial(_attn_a_body, bounded),
        grid=(batch, A_HEADS),
        in_specs=[head(COL_QA), head(COL_KA), head(COL_VA),
                  pl.BlockSpec((n_pat, None, BLK, 2 * BLK), lambda b, h: (0, h, 0, 0))],
        out_specs=pl.BlockSpec((SEQ, HEAD_DIM), lambda b, h: (b, h)),
        out_shape=jax.ShapeDtypeStruct((batch * SEQ, A_WIDTH), BF16),
        scratch_shapes=[pltpu.VMEM((3, SEQ, HEAD_DIM), F32)] * 2 + [
            pltpu.VMEM((n_stats, SEQ, HEAD_DIM), F32), pltpu.VMEM((n_stats, p3_rows, HEAD_DIM), F32)],
        compiler_params=pltpu.CompilerParams(
            dimension_semantics=("parallel", "parallel"), vmem_limit_bytes=VMEM_LIMIT),
        name="attn_dilated_bounded" if bounded else "attn_dilated",
    )(proj, proj, proj, bias)


FOX_TQ = 256


def _attn_b_body(bounded, q_ref, k_ref, v_ref, c_ref, ccol_ref, o_ref):
    h = pl.program_id(1)
    ck = c_ref[pl.ds(h, 1), :]
    qi = lax.broadcasted_iota(jnp.int32, (FOX_TQ, FOX_TQ), 0)
    kj = lax.broadcasted_iota(jnp.int32, (FOX_TQ, FOX_TQ), 1)
    causal = kj <= qi
    head_lane = lax.broadcasted_iota(jnp.int32, (FOX_TQ, LANES), 1) == h

    def task(i):
        lo, hi = i * FOX_TQ, (i + 1) * FOX_TQ

        def start():
            q = q_ref[lo:hi, :]
            s_diag = lax.dot_general(q, k_ref[lo:hi, :], _NT, preferred_element_type=F32)
            s_off = lax.dot_general(q, k_ref[:lo, :], _NT, preferred_element_type=F32) if i > 0 else None
            return s_diag, s_off

        def finish(s_diag, s_off):
            if bounded:
                cq = jnp.sum(jnp.where(head_lane, ccol_ref[lo:hi, :], 0.0), axis=-1, keepdims=True)
                p_diag = jnp.exp2(jnp.where(causal, (s_diag + cq) - ck[:, lo:hi], NEG))
                p_off = jnp.exp2((s_off + cq) - ck[:, :lo]) if i > 0 else None
            else:
                s_diag = jnp.where(causal, s_diag - ck[:, lo:hi], NEG)
                m = jnp.max(s_diag, axis=-1, keepdims=True)
                if i > 0:
                    s_off = s_off - ck[:, :lo]
                    m = jnp.maximum(m, jnp.max(s_off, axis=-1, keepdims=True))
                p_diag = jnp.exp2(s_diag - m)
                p_off = jnp.exp2(s_off - m) if i > 0 else None
            l = jnp.sum(p_diag, axis=-1, keepdims=True)
            acc = jnp.dot(p_diag.astype(BF16), v_ref[lo:hi, :], preferred_element_type=F32)
            if i > 0:
                l = l + jnp.sum(p_off, axis=-1, keepdims=True)
                acc = acc + jnp.dot(p_off.astype(BF16), v_ref[:lo, :], preferred_element_type=F32)
            o_ref[lo:hi, :] = (acc * (1.0 / l)).astype(BF16)

        return start, finish

    _emit_pipelined([task(i) for i in range(SEQ // FOX_TQ)], 2)


def _attn_b(proj, c, ccol, batch, bounded):
    head = lambda col0: pl.BlockSpec((SEQ, HEAD_DIM), lambda b, h: (b, col0 + h))
    return pl.pallas_call(
        functools.partial(_attn_b_body, bounded),
        grid=(batch, B_HEADS),
        in_specs=[head(COL_QB), head(COL_KB), head(COL_VB),
                  pl.BlockSpec((None, B_HEADS, SEQ), lambda b, h: (b, 0, 0)),
                  pl.BlockSpec((SEQ, LANES), lambda b, h: (b, 0))],
        out_specs=pl.BlockSpec((SEQ, HEAD_DIM), lambda b, h: (b, h)),
        out_shape=jax.ShapeDtypeStruct((batch * SEQ, B_WIDTH), BF16),
        compiler_params=pltpu.CompilerParams(
            dimension_semantics=("parallel", "parallel"), vmem_limit_bytes=VMEM_LIMIT),
        name="attn_forgetting_bounded" if bounded else "attn_forgetting",
    )(proj, proj, proj, c, ccol)


def _mem_kv_body(mem_ref, g_ref, w_ref, gain_ref, out_ref):
    hb = (_rms_scale(mem_ref[...]) * g_ref[...]).astype(BF16)
    acc = jnp.dot(hb, w_ref[...], preferred_element_type=F32)
    gain = gain_ref[...]
    for s in range(M_HEADS):
        sl = slice(s * M_HEAD_DIM, (s + 1) * M_HEAD_DIM)
        out_ref[:, sl] = (_rms_scale(acc[:, sl]) * gain[:, sl]).astype(BF16)
    out_ref[:, M_WIDTH:] = acc[:, M_WIDTH:].astype(BF16)


def _mem_kv(mem2, g, w, gain, batch):
    return pl.pallas_call(
        _mem_kv_body,
        grid=(batch,),
        in_specs=[
            pl.BlockSpec((MEM_LEN, D_MODEL), lambda b: (b, 0)),
            pl.BlockSpec((1, D_MODEL), lambda b: (0, 0)),
            pl.BlockSpec((D_MODEL, 2 * M_WIDTH), lambda b: (0, 0)),
            pl.BlockSpec((1, M_WIDTH), lambda b: (0, 0)),
        ],
        out_specs=pl.BlockSpec((MEM_LEN, 2 * M_WIDTH), lambda b: (b, 0)),
        out_shape=jax.ShapeDtypeStruct((batch * MEM_LEN, 2 * M_WIDTH), BF16),
        compiler_params=pltpu.CompilerParams(
            dimension_semantics=("parallel",), vmem_limit_bytes=VMEM_LIMIT),
        name="mem_kv",
    )(mem2, g, w, gain)


MEM_TQ = 512


def _attn_m_body(bounded, q_ref, kv_ref, o_ref):
    def task(h, i):
        rows = slice(i * MEM_TQ, (i + 1) * MEM_TQ)
        cols = slice(h * M_HEAD_DIM, (h + 1) * M_HEAD_DIM)
        v_cols = slice(M_WIDTH + h * M_HEAD_DIM, M_WIDTH + (h + 1) * M_HEAD_DIM)

        def finish(s):
            acc, l, _ = _softmax_pv(s, kv_ref[:, v_cols], bounded)
            o_ref[rows, cols] = (acc * (1.0 / l)).astype(BF16)

        return lambda: (lax.dot_general(q_ref[rows, cols], kv_ref[:, cols], _NT,
                                        preferred_element_type=F32),), finish

    _emit_pipelined([task(h, i) for h in range(M_HEADS) for i in range(SEQ // MEM_TQ)], 2)


def _attn_m(proj, kv, batch, bounded):
    assert (COL_QM * LANES) % M_WIDTH == 0
    return pl.pallas_call(
        functools.partial(_attn_m_body, bounded),
        grid=(batch,),
        in_specs=[
            pl.BlockSpec((SEQ, M_WIDTH), lambda b: (b, COL_QM * LANES // M_WIDTH)),
            pl.BlockSpec((MEM_LEN, 2 * M_WIDTH), lambda b: (b, 0)),
        ],
        out_specs=pl.BlockSpec((SEQ, M_WIDTH), lambda b: (b, 0)),
        out_shape=jax.ShapeDtypeStruct((batch * SEQ, M_WIDTH), BF16),
        compiler_params=pltpu.CompilerParams(
            dimension_semantics=("parallel",), vmem_limit_bytes=VMEM_LIMIT),
        name="attn_memory_bounded" if bounded else "attn_memory",
    )(proj, kv)


MERGE_TM = 256


def _merge_body(x_ref, oa_ref, za_ref, ob_ref, zb_ref, om_ref, zm_ref, gla_ref, glb_ref, glm_ref, glt_ref, bg_ref,
                wa_ref, wb_ref, wm_ref, wo_ref, out_ref):
    def branch(o_ref, z_ref, w_ref):
        z = z_ref[...].astype(F32)
        gated = o_ref[...].astype(F32) * (z * jax.nn.sigmoid(z))
        return jnp.dot(gated.astype(BF16), w_ref[...], preferred_element_type=F32)

    gate_logits = (lambda: gla_ref[...], lambda: glb_ref[...],
                   lambda: jnp.concatenate([glm_ref[...], glt_ref[...]], axis=1))
    merged = None
    for n, (o_ref, z_ref, w_ref) in enumerate(((oa_ref, za_ref, wa_ref), (ob_ref, zb_ref, wb_ref),
                                                (om_ref, zm_ref, wm_ref))):
        cols = slice(n * D_MODEL, (n + 1) * D_MODEL)
        gate = jax.nn.sigmoid(gate_logits[n]().astype(F32) + bg_ref[:, cols])
        term = gate * branch(o_ref, z_ref, w_ref)
        merged = term if merged is None else merged + term
    out_ref[...] = x_ref[...] + jnp.dot(merged.astype(BF16), wo_ref[...], preferred_element_type=F32)


def _merge(x2, oa, ob, om, proj, gl_tail, bg, wa, wb, wm, wo):
    rows = x2.shape[0]
    row_blk = lambda width, colblk: pl.BlockSpec((MERGE_TM, width), lambda i: (i, colblk))
    resident = lambda shape: pl.BlockSpec(shape, lambda i: (0, 0), pipeline_mode=pl.Buffered(1))
    assert (COL_ZA * LANES) % A_WIDTH == 0 and (COL_ZB * LANES) % B_WIDTH == 0
    assert (COL_ZM * LANES) % M_WIDTH == 0 and (COL_GL * LANES) % D_MODEL == 0
    assert gl_tail.shape[1] == PROJ_TN and D_MODEL % PROJ_TN == 0
    gl_col = COL_GL * LANES // D_MODEL
    gl_m_width = D_MODEL - PROJ_TN
    return pl.pallas_call(
        _merge_body,
        grid=(rows // MERGE_TM,),
        in_specs=[
            row_blk(D_MODEL, 0),
            row_blk(A_WIDTH, 0), row_blk(A_WIDTH, COL_ZA * LANES // A_WIDTH),
            row_blk(B_WIDTH, 0), row_blk(B_WIDTH, COL_ZB * LANES // B_WIDTH),
            row_blk(M_WIDTH, 0), row_blk(M_WIDTH, COL_ZM * LANES // M_WIDTH),
            row_blk(D_MODEL, gl_col), row_blk(D_MODEL, gl_col + 1),
            row_blk(gl_m_width, (gl_col + 2) * D_MODEL // gl_m_width), row_blk(PROJ_TN, 0),
            resident((1, N_BRANCH * D_MODEL)),
            resident((A_WIDTH, D_MODEL)), resident((B_WIDTH, D_MODEL)), resident((M_WIDTH, D_MODEL)),
            resident((D_MODEL, D_MODEL)),
        ],
        out_specs=row_blk(D_MODEL, 0),
        out_shape=jax.ShapeDtypeStruct((rows, D_MODEL), F32),
        compiler_params=pltpu.CompilerParams(
            dimension_semantics=("parallel",), vmem_limit_bytes=VMEM_LIMIT),
        name="merge_out",
    )(x2, oa, proj, ob, proj, om, proj, proj, proj, proj, gl_tail, bg, wa, wb, wm, wo)


BRANCH_TM = 512
BRANCH_SPLIT = 2
OUT_TM = 512
OUT_CHUNK = 256


def _branch_merge_body(oa_ref, za_ref, ob_ref, zb_ref, om_ref, zm_ref, gla_ref, glb_ref, glm_ref, glt_ref, bg_ref,
                       wa_ref, wb_ref, wm_ref, out_ref, acc_ref):
    half = D_MODEL // BRANCH_SPLIT
    branches = ((oa_ref, za_ref, wa_ref), (ob_ref, zb_ref, wb_ref), (om_ref, zm_ref, wm_ref))
    if BRANCH_SPLIT == 1:
        last = lambda c: jnp.concatenate([glm_ref[...], glt_ref[...]], axis=1)
    else:
        last = lambda c: glm_ref[...] if c.start == 0 else glt_ref[...]
    gate_logits = ((lambda c: gla_ref[:, c]), (lambda c: glb_ref[:, c]), last)
    gated = {}

    def task(n, hf):
        o_ref, z_ref, w_ref = branches[n]
        cols = slice(hf * half, (hf + 1) * half)

        def start():
            if n not in gated:
                z = z_ref[...].astype(F32)
                gated[n] = (o_ref[...].astype(F32) * (z * jax.nn.sigmoid(z))).astype(BF16)
            return (jnp.dot(gated[n], w_ref[:, cols], preferred_element_type=F32),)

        def finish(u):
            bias = bg_ref[:, n * D_MODEL + hf * half:n * D_MODEL + (hf + 1) * half]
            term = jax.nn.sigmoid(gate_logits[n](cols).astype(F32) + bias) * u
            if n == 0:
                acc_ref[:, cols] = term
            elif n < N_BRANCH - 1:
                acc_ref[:, cols] += term
            else:
                out_ref[:, cols] = (acc_ref[:, cols] + term).astype(BF16)

        return start, finish

    _emit_pipelined([task(n, hf) for n in range(N_BRANCH) for hf in range(BRANCH_SPLIT)], 2)


def _branch_merge(oa, ob, om, proj, gl_tail, bg, wa, wb, wm):
    rows = oa.shape[0]
    row_blk = lambda width, colblk: pl.BlockSpec((BRANCH_TM, width), lambda i: (i, colblk))
    resident = lambda shape: pl.BlockSpec(shape, lambda i: (0, 0), pipeline_mode=pl.Buffered(1))
    assert (COL_ZA * LANES) % A_WIDTH == 0 and (COL_ZB * LANES) % B_WIDTH == 0
    assert (COL_ZM * LANES) % M_WIDTH == 0 and (COL_GL * LANES) % D_MODEL == 0
    assert gl_tail.shape[1] == PROJ_TN == D_MODEL // 2
    gl_col = COL_GL * LANES // D_MODEL
    return pl.pallas_call(
        _branch_merge_body,
        grid=(rows // BRANCH_TM,),
        in_specs=[
            row_blk(A_WIDTH, 0), row_blk(A_WIDTH, COL_ZA * LANES // A_WIDTH),
            row_blk(B_WIDTH, 0), row_blk(B_WIDTH, COL_ZB * LANES // B_WIDTH),
            row_blk(M_WIDTH, 0), row_blk(M_WIDTH, COL_ZM * LANES // M_WIDTH),
            row_blk(D_MODEL, gl_col), row_blk(D_MODEL, gl_col + 1),
            row_blk(PROJ_TN, (gl_col + 2) * D_MODEL // PROJ_TN), row_blk(PROJ_TN, 0),
            resident((1, N_BRANCH * D_MODEL)),
            resident((A_WIDTH, D_MODEL)), resident((B_WIDTH, D_MODEL)), resident((M_WIDTH, D_MODEL)),
        ],
        out_specs=row_blk(D_MODEL, 0),
        out_shape=jax.ShapeDtypeStruct((rows, D_MODEL), BF16),
        scratch_shapes=[pltpu.VMEM((BRANCH_TM, D_MODEL), F32)],
        compiler_params=pltpu.CompilerParams(
            dimension_semantics=("parallel",), vmem_limit_bytes=VMEM_LIMIT),
        name="branch_merge",
    )(oa, proj, ob, proj, om, proj, proj, proj, proj, gl_tail, bg, wa, wb, wm)


def _out_proj_body(x_ref, m_ref, wo_ref, out_ref):
    def task(r):
        rows = slice(r * OUT_CHUNK, (r + 1) * OUT_CHUNK)

        def finish(acc):
            out_ref[rows, :] = x_ref[rows, :] + acc

        return lambda: (jnp.dot(m_ref[rows, :], wo_ref[...], preferred_element_type=F32),), finish

    _emit_pipelined([task(r) for r in range(OUT_TM // OUT_CHUNK)], 1)


def _out_proj(x2, merged, wo):
    rows = x2.shape[0]
    row_blk = pl.BlockSpec((OUT_TM, D_MODEL), lambda i: (i, 0))
    return pl.pallas_call(
        _out_proj_body,
        grid=(rows // OUT_TM,),
        in_specs=[row_blk, row_blk,
                  pl.BlockSpec((D_MODEL, D_MODEL), lambda i: (0, 0), pipeline_mode=pl.Buffered(1))],
        out_specs=row_blk,
        out_shape=jax.ShapeDtypeStruct((rows, D_MODEL), F32),
        compiler_params=pltpu.CompilerParams(
            dimension_semantics=("parallel",), vmem_limit_bytes=VMEM_LIMIT),
        name="out_proj",
    )(x2, merged, wo)


def _layer(x2, mem2, batch, layer, norm_g, mem_norm_g, w_in, b_forget, b_gate, rel_bias, q_norm_a, k_norm_a,
           q_norm_b, k_norm_b, q_norm_m, k_norm_m, w_mem_kv, w_proj_a, w_proj_b, w_proj_m, w_out):
    b_f = jnp.pad(b_forget.astype(F32), (0, LANES - B_HEADS))[None]
    scale = LOG2E / math.sqrt(HEAD_DIM)
    scale_m = LOG2E / math.sqrt(M_HEAD_DIM)
    ones = lambda n: jnp.ones((n,), F32)
    gain = jnp.concatenate([
        jnp.tile(q_norm_a.astype(F32) * scale, A_HEADS), jnp.tile(k_norm_a.astype(F32), A_HEADS), ones(2 * A_WIDTH),
        jnp.tile(q_norm_b.astype(F32) * scale, B_HEADS), jnp.tile(k_norm_b.astype(F32), B_HEADS), ones(2 * B_WIDTH),
        jnp.tile(q_norm_m.astype(F32) * scale_m, M_HEADS), ones(PROJ_COLS - COL_ZM * LANES)])[None]

    w_t = jnp.swapaxes(w_in.astype(F32), 1, 2)
    h, logf, gl_tail = _norm(x2, norm_g.astype(F32)[None], w_t, layer, b_f)
    proj = _proj(h, w_t, layer, gain)

    def logit_bound(gq, gk, dim, scl):
        return dim * scl * jnp.max(jnp.abs(gq.astype(F32))) * jnp.max(jnp.abs(gk.astype(F32)))

    def dispatch(bound, call, *operands):
        return lax.cond(bound <= LOGIT_RANGE, functools.partial(call, bounded=True),
                        functools.partial(call, bounded=False), *operands)

    bound_a = logit_bound(q_norm_a, k_norm_a, HEAD_DIM, scale) + LOG2E * jnp.max(jnp.abs(rel_bias.astype(F32)))
    oa = dispatch(bound_a, lambda p, b, bounded: _attn_a(p, b, batch, bounded),
                  proj, _bias_tiles(rel_bias.astype(F32)))

    c, ccol = _cumsum(logf, batch)
    ob = dispatch(logit_bound(q_norm_b, k_norm_b, HEAD_DIM, scale),
                  lambda p, c_, cc, bounded: _attn_b(p, c_, cc, batch, bounded), proj, c, ccol)

    kv = _mem_kv(mem2, mem_norm_g.astype(F32)[None], w_mem_kv.astype(BF16),
                 jnp.tile(k_norm_m.astype(F32), M_HEADS)[None], batch)
    om = dispatch(logit_bound(q_norm_m, k_norm_m, M_HEAD_DIM, scale_m),
                  lambda p, kv_, bounded: _attn_m(p, kv_, batch, bounded), proj, kv)

    merged = _branch_merge(oa, ob, om, proj, gl_tail, b_gate.astype(F32).reshape(1, N_BRANCH * D_MODEL),
                           w_proj_a.astype(BF16), w_proj_b.astype(BF16), w_proj_m.astype(BF16))
    return _out_proj(x2, merged, w_out.astype(BF16))


def kernel(x, mem, norm_g, mem_norm_g, w_in, b_forget, b_gate, rel_bias, q_norm_a, k_norm_a, q_norm_b, k_norm_b,
           q_norm_m, k_norm_m, w_mem_kv, w_proj_a, w_proj_b, w_proj_m, w_out):
    batch, seq, d_model = x.shape
    assert (seq, d_model) == (SEQ, D_MODEL) and mem.shape == (batch, MEM_LEN, D_MODEL)
    x2 = x.reshape(batch * seq, d_model)
    mem2 = mem.reshape(batch * MEM_LEN, d_model)
    depth = w_in.shape[0]
    for l in range(depth):
        x2 = _layer(x2, mem2, batch, l, norm_g[l], mem_norm_g[l], w_in, b_forget[l], b_gate[l], rel_bias,
                    q_norm_a[l], k_norm_a[l], q_norm_b[l], k_norm_b[l], q_norm_m[l], k_norm_m[l],
                    w_mem_kv[l], w_proj_a[l], w_proj_b[l], w_proj_m[l], w_out[l])
    return x2.reshape(batch, seq, d_model)
```

```python
import functools
import math

import numpy as np
import jax
import jax.numpy as jnp
from jax import lax
from jax.experimental import pallas as pl
from jax.experimental.pallas import tpu as pltpu

F32 = jnp.float32
BF16 = jnp.bfloat16

D_MODEL = 2048
SEQ = 2048
HEAD_DIM = 128
A_HEADS = 12
B_HEADS = 8
M_HEADS = 4
M_HEAD_DIM = 256
MEM_LEN = 256
A_WIDTH = A_HEADS * HEAD_DIM
B_WIDTH = B_HEADS * HEAD_DIM
M_WIDTH = M_HEADS * M_HEAD_DIM
N_BRANCH = 3
BLK = 128
NBLK = SEQ // BLK
DILATED_PATTERNS = ((128, 1), (512, 4), (2048, 16))
REL_BUCKETS = 32
REL_MAX_DIST = 2048
EPS = 1e-6
NEG = -1e30
LOG2E = math.log2(math.e)

LANES = 128
VMEM_LIMIT = 56 * 1024 * 1024

COL_QA = 0
COL_KA = COL_QA + A_HEADS
COL_VA = COL_KA + A_HEADS
COL_ZA = COL_VA + A_HEADS
COL_QB = COL_ZA + A_HEADS
COL_KB = COL_QB + B_HEADS
COL_VB = COL_KB + B_HEADS
COL_ZB = COL_VB + B_HEADS
COL_QM = COL_ZB + B_HEADS
COL_ZM = COL_QM + M_WIDTH // LANES
COL_GL = COL_ZM + M_WIDTH // LANES
COL_END = COL_GL + N_BRANCH * D_MODEL // LANES
PROJ_COLS = COL_END * LANES

NORM_TM = 1024
PROJ_TM = 2048
PROJ_TN = 1024
PROJ_CHUNK = 256
PROJ_NB_HEAD = COL_QM * LANES // PROJ_TN
F_SHIFT = B_HEADS


_NT = (((1,), (1,)), ((), ()))


def _log_sigmoid(x):
    return jnp.minimum(x, 0.0) - jnp.log1p(jnp.exp(-jnp.abs(x)))


def _rms_scale(a):
    return a * lax.rsqrt(jnp.mean(a * a, axis=-1, keepdims=True) + EPS)


def _emit_pipelined(tasks, lookahead):
    pending = []
    for start, finish in tasks:
        pending.append((finish, start()))
        if len(pending) > lookahead:
            fin, state = pending.pop(0)
            fin(*state)
    for fin, state in pending:
        fin(*state)


def _col_block_range(col0, width):
    return col0 * LANES // PROJ_TN, (col0 * LANES + width) // PROJ_TN


def _in_col_blocks(j, *ranges):
    hit = None
    for lo, hi in ranges:
        cond = (j >= lo) & (j < hi)
        hit = cond if hit is None else hit | cond
    return hit


def _head_normed(out_ref, gain_ref, hd):
    def epilogue(acc, rows):
        gain = gain_ref[...]
        for s in range(PROJ_TN // hd):
            sl = slice(s * hd, (s + 1) * hd)
            out_ref[rows, sl] = (_rms_scale(acc[:, sl]) * gain[:, sl]).astype(BF16)
    return epilogue


def _norm_body(x_ref, g_ref, wf_ref, bf_ref, wa_ref, wb_ref, h_ref, f_ref, p_ref, w16_ref):
    @pl.when(pl.program_id(0) == 0)
    def _():
        w16_ref[...] = jnp.concatenate([wa_ref[F_SHIFT:, :], wb_ref[...]], axis=0).astype(BF16)

    wf = jnp.concatenate([wf_ref[...], jnp.zeros((LANES - B_HEADS, D_MODEL), F32)], axis=0).astype(BF16)

    def epilogue(acc, rows):
        p_ref[rows, :] = acc.astype(BF16)

    tasks = []
    for r in range(NORM_TM // PROJ_CHUNK):
        rows = slice(r * PROJ_CHUNK, (r + 1) * PROJ_CHUNK)

        def start(rows=rows):
            hb = (_rms_scale(x_ref[rows, :]) * g_ref[...]).astype(BF16)
            h_ref[rows, :] = hb
            f = lax.dot_general(hb, wf, _NT, preferred_element_type=F32) + bf_ref[...]
            f_ref[rows, :] = _log_sigmoid(f)
            return (lax.dot_general(hb, w16_ref[...], _NT, preferred_element_type=F32),)

        tasks.append((start, lambda acc, rows=rows: epilogue(acc, rows)))
    _emit_pipelined(tasks, 2)


def _norm(x2, g, w_t, layer, bfp):
    rows = x2.shape[0]
    last = PROJ_COLS // PROJ_TN - 1
    return pl.pallas_call(
        _norm_body,
        grid=(rows // NORM_TM,),
        in_specs=[
            pl.BlockSpec((NORM_TM, D_MODEL), lambda i: (i, 0)),
            pl.BlockSpec((1, D_MODEL), lambda i: (0, 0)),
            pl.BlockSpec((None, B_HEADS, D_MODEL), lambda i: (layer, COL_QM * LANES // B_HEADS, 0)),
            pl.BlockSpec((1, LANES), lambda i: (0, 0)),
            pl.BlockSpec((None, PROJ_TN, D_MODEL), lambda i: (layer, last, 0), pipeline_mode=pl.Buffered(1)),
            pl.BlockSpec((None, F_SHIFT, D_MODEL), lambda i: (layer, (last + 1) * (PROJ_TN // F_SHIFT), 0)),
        ],
        out_specs=[pl.BlockSpec((NORM_TM, D_MODEL), lambda i: (i, 0)),
                   pl.BlockSpec((NORM_TM, LANES), lambda i: (i, 0)),
                   pl.BlockSpec((NORM_TM, PROJ_TN), lambda i: (i, 0))],
        out_shape=[jax.ShapeDtypeStruct((rows, D_MODEL), BF16), jax.ShapeDtypeStruct((rows, LANES), F32),
                   jax.ShapeDtypeStruct((rows, PROJ_TN), BF16)],
        scratch_shapes=[pltpu.VMEM((PROJ_TN, D_MODEL), BF16)],
        compiler_params=pltpu.CompilerParams(dimension_semantics=("arbitrary",), vmem_limit_bytes=VMEM_LIMIT),
        name="norm_proj_last",
    )(x2, g, w_t, bfp, w_t, w_t)


def _proj_body(h_ref, wa_ref, wb_ref, gain_ref, out_ref, w16_ref):
    j = pl.program_id(0)

    @pl.when((pl.program_id(1) == 0) & (j < PROJ_NB_HEAD))
    def _():
        w16_ref[...] = wa_ref[...].astype(BF16)

    @pl.when((pl.program_id(1) == 0) & (j >= PROJ_NB_HEAD))
    def _():
        w16_ref[...] = jnp.concatenate([wa_ref[F_SHIFT:, :], wb_ref[...]], axis=0).astype(BF16)

    def plain(acc, rows):
        out_ref[rows, :] = acc.astype(BF16)

    def run(epilogue):
        tasks = []
        for r in range(PROJ_TM // PROJ_CHUNK):
            rows = slice(r * PROJ_CHUNK, (r + 1) * PROJ_CHUNK)
            tasks.append((lambda rows=rows: (lax.dot_general(h_ref[rows, :], w16_ref[...], _NT,
                                                             preferred_element_type=F32),),
                          lambda acc, rows=rows: epilogue(acc, rows)))
        _emit_pipelined(tasks, 1)

    head128 = _in_col_blocks(j, _col_block_range(COL_QA, 2 * A_WIDTH), _col_block_range(COL_QB, 2 * B_WIDTH))
    head256 = _in_col_blocks(j, _col_block_range(COL_QM, M_WIDTH))
    pl.when(head128)(lambda: run(_head_normed(out_ref, gain_ref, HEAD_DIM)))
    pl.when(head256)(lambda: run(_head_normed(out_ref, gain_ref, M_HEAD_DIM)))
    pl.when(jnp.logical_not(head128 | head256))(lambda: run(plain))


def _proj(h, w_t, layer, gain):
    rows = h.shape[0]
    n_blocks = PROJ_COLS // PROJ_TN - 1

    def w_main(j, i):
        return layer, j, 0

    def w_spill(j, i):
        return layer, (jnp.maximum(j, PROJ_NB_HEAD) + 1) * (PROJ_TN // F_SHIFT), 0

    return pl.pallas_call(
        _proj_body,
        grid=(n_blocks, rows // PROJ_TM),
        in_specs=[
            pl.BlockSpec((PROJ_TM, D_MODEL), lambda j, i: (i, 0)),
            pl.BlockSpec((None, PROJ_TN, D_MODEL), w_main),
            pl.BlockSpec((None, F_SHIFT, D_MODEL), w_spill),
            pl.BlockSpec((1, PROJ_TN), lambda j, i: (0, j)),
        ],
        out_specs=pl.BlockSpec((PROJ_TM, PROJ_TN), lambda j, i: (i, j)),
        out_shape=jax.ShapeDtypeStruct((rows, n_blocks * PROJ_TN), BF16),
        scratch_shapes=[pltpu.VMEM((PROJ_TN, D_MODEL), BF16)],
        compiler_params=pltpu.CompilerParams(
            dimension_semantics=("parallel", "arbitrary"), vmem_limit_bytes=VMEM_LIMIT),
        name="proj",
    )(h, w_t, w_t, gain)


def _rel_bucket_np(dist):
    max_exact = REL_BUCKETS // 2
    d_f = np.maximum(dist, 1).astype(np.float32)
    large = max_exact + (np.log(d_f / np.float32(max_exact)) / np.float32(math.log(REL_MAX_DIST / max_exact))
                         * np.float32(REL_BUCKETS - max_exact)).astype(np.int32)
    large = np.minimum(large, REL_BUCKETS - 1)
    return np.where(dist < max_exact, dist, large).astype(np.int32)


def _bias_buckets():
    qi = np.arange(BLK)[:, None]
    kj = np.arange(2 * BLK)[None, :]
    delta = qi - kj + BLK
    tiles = []
    for window, dil in DILATED_PATTERNS:
        w_sub = window // dil
        valid = (delta >= 0) & (delta <= w_sub)
        tiles.append(np.where(valid, _rel_bucket_np(np.clip(delta, 0, w_sub) * dil), -1))
    return np.stack(tiles).astype(np.int32)


def _bias_body(tbl_ref, bucket_ref, out_ref):
    bucket = bucket_ref[...]
    for h in range(A_HEADS):
        out = jnp.full(bucket.shape, NEG, F32)
        for r in range(REL_BUCKETS):
            out = jnp.where(bucket == r, tbl_ref[r, h] * LOG2E, out)
        out_ref[h] = out


def _bias_tiles(rel_bias):
    n_pat = len(DILATED_PATTERNS)
    return pl.pallas_call(
        _bias_body,
        grid=(n_pat,),
        in_specs=[pl.BlockSpec(memory_space=pltpu.SMEM),
                  pl.BlockSpec((None, BLK, 2 * BLK), lambda p: (p, 0, 0))],
        out_specs=pl.BlockSpec((None, A_HEADS, BLK, 2 * BLK), lambda p: (p, 0, 0, 0)),
        out_shape=jax.ShapeDtypeStruct((n_pat, A_HEADS, BLK, 2 * BLK), F32),
        name="rel_bias_tiles",
    )(rel_bias, jnp.asarray(_bias_buckets()))


def _cumsum_body(f_ref, c_ref, ccol_ref):
    ft = f_ref[...].T[:B_HEADS, :]
    pos = lax.broadcasted_iota(jnp.int32, ft.shape, 1)
    shift = 1
    while shift < SEQ:
        ft = ft + jnp.where(pos >= shift, pltpu.roll(ft, shift, axis=1), 0.0)
        shift *= 2
    c = ft * LOG2E
    c_ref[...] = c
    ccol_ref[...] = jnp.concatenate([c, jnp.zeros((LANES - B_HEADS, SEQ), F32)], axis=0).T


def _cumsum(logf, batch):
    return pl.pallas_call(
        _cumsum_body,
        grid=(batch,),
        in_specs=[pl.BlockSpec((SEQ, LANES), lambda b: (b, 0))],
        out_specs=[pl.BlockSpec((None, B_HEADS, SEQ), lambda b: (b, 0, 0)),
                   pl.BlockSpec((SEQ, LANES), lambda b: (b, 0))],
        out_shape=[jax.ShapeDtypeStruct((batch, B_HEADS, SEQ), F32),
                   jax.ShapeDtypeStruct((batch * SEQ, LANES), F32)],
        name="forget_cumsum",
    )(logf)


P3_PITCH = BLK + 8


def _scores(q, k, bias):
    return lax.dot_general(q, k, _NT, preferred_element_type=F32) + bias


def _softmax_pv(s, v, bounded):
    m = 0.0 if bounded else jnp.max(s, axis=-1, keepdims=True)
    pe = jnp.exp2(s) if bounded else jnp.exp2(s - m)
    l = jnp.sum(pe, axis=-1, keepdims=True)
    return jnp.dot(pe.astype(BF16), v, preferred_element_type=F32), l, m


LOGIT_RANGE = 60.0
ATTN_A_LOOKAHEAD = {True: 4, False: 8}


def _attn_a_body(bounded, q_ref, k_ref, v_ref, bias_ref, o_ref, nat, cm4, res2, res3):
    dil4, dil16 = DILATED_PATTERNS[1][1], DILATED_PATTERNS[2][1]
    cls_len = SEQ // dil4
    for i, ref in enumerate((q_ref, k_ref, v_ref)):
        nat[i] = ref[...].astype(F32)
        for c in range(dil4):
            cm4[i, c * cls_len:(c + 1) * cls_len, :] = nat[i, pl.ds(c, cls_len, stride=dil4), :]

    def operands(q_rows, kv_rows):
        return (cm4[0, q_rows, :].astype(BF16), cm4[1, kv_rows, :].astype(BF16), cm4[2, kv_rows, :].astype(BF16))

    def save(res, rows, acc, l, m):
        res[0, rows, :] = acc
        res[1, rows, :] = jnp.broadcast_to(l, acc.shape)
        if not bounded:
            res[2, rows, :] = jnp.broadcast_to(m, acc.shape)

    tasks = []

    def add_task(q_rows, kv_rows, bias, res, out_rows):
        def start():
            q, k, v = operands(q_rows, kv_rows)
            return _scores(q, k, bias()), v

        tasks.append((start, lambda s, v: save(res, out_rows, *_softmax_pv(s, v, bounded))))

    for c in range(dil4):
        for n in range(cls_len // BLK):
            lo = c * cls_len + n * BLK
            out_rows = pl.ds(n * BLK * dil4 + c, BLK, stride=dil4)
            if n == 0:
                add_task(slice(lo, lo + BLK), slice(lo, lo + BLK), lambda: bias_ref[1, :, BLK:], res2, out_rows)
            else:
                add_task(slice(lo, lo + BLK), slice(lo - BLK, lo + BLK), lambda: bias_ref[1], res2, out_rows)

    for c in range(dil16):
        rows = pl.ds((c % dil4) * cls_len + c // dil4, BLK, stride=dil16 // dil4)
        add_task(rows, rows, lambda: bias_ref[2, :, BLK:], res3, slice(c * P3_PITCH, c * P3_PITCH + BLK))

    def token_order(i, g):
        per_class = BLK // dil16
        return jnp.concatenate(
            [res3[i, pl.ds(g * per_class + j, dil16, stride=P3_PITCH), :] for j in range(per_class)], axis=0)

    def add_merge_task(g):
        rows = slice(g * BLK, (g + 1) * BLK)
        keys = rows if g == 0 else slice((g - 1) * BLK, (g + 1) * BLK)

        def start():
            bias = bias_ref[0, :, BLK:] if g == 0 else bias_ref[0]
            return _scores(q_ref[rows, :], k_ref[keys, :], bias), v_ref[keys, :]

        def finish(s, v):
            acc1, l1, m1 = _softmax_pv(s, v, bounded)
            if bounded:
                num = acc1 + res2[0, rows, :] + token_order(0, g)
                den = l1 + res2[1, rows, :] + token_order(1, g)
            else:
                m2, m3 = res2[2, rows, :], token_order(2, g)
                top = jnp.maximum(jnp.maximum(m1, m2), m3)
                w1, w2, w3 = jnp.exp2(m1 - top), jnp.exp2(m2 - top), jnp.exp2(m3 - top)
                num = w1 * acc1 + w2 * res2[0, rows, :] + w3 * token_order(0, g)
                den = w1 * l1 + w2 * res2[1, rows, :] + w3 * token_order(1, g)
            o_ref[rows, :] = (num * (1.0 / den)).astype(BF16)

        tasks.append((start, finish))

    for g in range(NBLK):
        add_merge_task(g)
    _emit_pipelined(tasks, ATTN_A_LOOKAHEAD[bounded])


def _attn_a(proj, bias, batch, bounded):
    n_pat = len(DILATED_PATTERNS)
    head = lambda col0: pl.BlockSpec((SEQ, HEAD_DIM), lambda b, h: (b, col0 + h))
    p3_rows = DILATED_PATTERNS[2][1] * P3_PITCH
    n_stats = 2 if bounded else 3
    return pl.pallas_call(
        functools.partial(_attn_a_body, bounded),
        grid=(batch, A_HEADS),
        in_specs=[head(COL_QA), head(COL_KA), head(COL_VA),
                  pl.BlockSpec((n_pat, None, BLK, 2 * BLK), lambda b, h: (0, h, 0, 0))],
        out_specs=pl.BlockSpec((SEQ, HEAD_DIM), lambda b, h: (b, h)),
        out_shape=jax.ShapeDtypeStruct((batch * SEQ, A_WIDTH), BF16),
        scratch_shapes=[pltpu.VMEM((3, SEQ, HEAD_DIM), F32)] * 2 + [
            pltpu.VMEM((n_stats, SEQ, HEAD_DIM), F32), pltpu.VMEM((n_stats, p3_rows, HEAD_DIM), F32)],
        compiler_params=pltpu.CompilerParams(
            dimension_semantics=("parallel", "parallel"), vmem_limit_bytes=VMEM_LIMIT),
        name="attn_dilated_bounded" if bounded else "attn_dilated",
    )(proj, proj, proj, bias)


FOX_TQ = 256
FOX_HEADS = 2


def _attn_b_body(bounded, q_ref, k_ref, v_ref, c_ref, ccol_ref, o_ref):
    qi = lax.broadcasted_iota(jnp.int32, (FOX_TQ, FOX_TQ), 0)
    kj = lax.broadcasted_iota(jnp.int32, (FOX_TQ, FOX_TQ), 1)
    causal = kj <= qi

    def task(hh, i):
        lo, hi = i * FOX_TQ, (i + 1) * FOX_TQ
        hd = slice(hh * HEAD_DIM, (hh + 1) * HEAD_DIM)
        h = pl.program_id(1) * FOX_HEADS + hh
        ck = c_ref[pl.ds(h, 1), :]
        head_lane = lax.broadcasted_iota(jnp.int32, (FOX_TQ, LANES), 1) == h

        def start():
            q = q_ref[lo:hi, hd]
            s_diag = lax.dot_general(q, k_ref[lo:hi, hd], _NT, preferred_element_type=F32)
            s_off = lax.dot_general(q, k_ref[:lo, hd], _NT, preferred_element_type=F32) if i > 0 else None
            return s_diag, s_off

        def finish(s_diag, s_off):
            if bounded:
                cq = jnp.sum(jnp.where(head_lane, ccol_ref[lo:hi, :], 0.0), axis=-1, keepdims=True)
                p_diag = jnp.exp2(jnp.where(causal, (s_diag + cq) - ck[:, lo:hi], NEG))
                p_off = jnp.exp2((s_off + cq) - ck[:, :lo]) if i > 0 else None
            else:
                s_diag = jnp.where(causal, s_diag - ck[:, lo:hi], NEG)
                m = jnp.max(s_diag, axis=-1, keepdims=True)
                if i > 0:
                    s_off = s_off - ck[:, :lo]
                    m = jnp.maximum(m, jnp.max(s_off, axis=-1, keepdims=True))
                p_diag = jnp.exp2(s_diag - m)
                p_off = jnp.exp2(s_off - m) if i > 0 else None
            l = jnp.sum(p_diag, axis=-1, keepdims=True)
            acc = jnp.dot(p_diag.astype(BF16), v_ref[lo:hi, hd], preferred_element_type=F32)
            if i > 0:
                l = l + jnp.sum(p_off, axis=-1, keepdims=True)
                acc = acc + jnp.dot(p_off.astype(BF16), v_ref[:lo, hd], preferred_element_type=F32)
            o_ref[lo:hi, hd] = (acc * (1.0 / l)).astype(BF16)

        return start, finish

    _emit_pipelined([task(hh, i) for hh in range(FOX_HEADS) for i in range(SEQ // FOX_TQ)], 2)


def _attn_b(proj, c, ccol, batch, bounded):
    width = FOX_HEADS * HEAD_DIM
    assert B_HEADS % FOX_HEADS == 0 and all(c0 % FOX_HEADS == 0 for c0 in (COL_QB, COL_KB, COL_VB))
    head = lambda col0: pl.BlockSpec((SEQ, width), lambda b, h: (b, col0 // FOX_HEADS + h))
    return pl.pallas_call(
        functools.partial(_attn_b_body, bounded),
        grid=(batch, B_HEADS // FOX_HEADS),
        in_specs=[head(COL_QB), head(COL_KB), head(COL_VB),
                  pl.BlockSpec((None, B_HEADS, SEQ), lambda b, h: (b, 0, 0)),
                  pl.BlockSpec((SEQ, LANES), lambda b, h: (b, 0))],
        out_specs=pl.BlockSpec((SEQ, width), lambda b, h: (b, h)),
        out_shape=jax.ShapeDtypeStruct((batch * SEQ, B_WIDTH), BF16),
        compiler_params=pltpu.CompilerParams(
            dimension_semantics=("parallel", "parallel"), vmem_limit_bytes=VMEM_LIMIT),
        name="attn_forgetting_bounded" if bounded else "attn_forgetting",
    )(proj, proj, proj, c, ccol)


def _mem_kv_body(mem_ref, g_ref, w_ref, gain_ref, out_ref):
    hb = (_rms_scale(mem_ref[...]) * g_ref[...]).astype(BF16)
    acc = jnp.dot(hb, w_ref[...], preferred_element_type=F32)
    gain = gain_ref[...]
    for s in range(M_HEADS):
        sl = slice(s * M_HEAD_DIM, (s + 1) * M_HEAD_DIM)
        out_ref[:, sl] = (_rms_scale(acc[:, sl]) * gain[:, sl]).astype(BF16)
    out_ref[:, M_WIDTH:] = acc[:, M_WIDTH:].astype(BF16)


def _mem_kv(mem2, g, w, gain, batch):
    return pl.pallas_call(
        _mem_kv_body,
        grid=(batch,),
        in_specs=[
            pl.BlockSpec((MEM_LEN, D_MODEL), lambda b: (b, 0)),
            pl.BlockSpec((1, D_MODEL), lambda b: (0, 0)),
            pl.BlockSpec((D_MODEL, 2 * M_WIDTH), lambda b: (0, 0)),
            pl.BlockSpec((1, M_WIDTH), lambda b: (0, 0)),
        ],
        out_specs=pl.BlockSpec((MEM_LEN, 2 * M_WIDTH), lambda b: (b, 0)),
        out_shape=jax.ShapeDtypeStruct((batch * MEM_LEN, 2 * M_WIDTH), BF16),
        compiler_params=pltpu.CompilerParams(
            dimension_semantics=("parallel",), vmem_limit_bytes=VMEM_LIMIT),
        name="mem_kv",
    )(mem2, g, w, gain)


MEM_TQ = 512


def _attn_m_body(bounded, q_ref, kv_ref, o_ref):
    def task(h, i):
        rows = slice(i * MEM_TQ, (i + 1) * MEM_TQ)
        cols = slice(h * M_HEAD_DIM, (h + 1) * M_HEAD_DIM)
        v_cols = slice(M_WIDTH + h * M_HEAD_DIM, M_WIDTH + (h + 1) * M_HEAD_DIM)

        def finish(s):
            acc, l, _ = _softmax_pv(s, kv_ref[:, v_cols], bounded)
            o_ref[rows, cols] = (acc * (1.0 / l)).astype(BF16)

        return lambda: (lax.dot_general(q_ref[rows, cols], kv_ref[:, cols], _NT,
                                        preferred_element_type=F32),), finish

    _emit_pipelined([task(h, i) for h in range(M_HEADS) for i in range(SEQ // MEM_TQ)], 2)


def _attn_m(proj, kv, batch, bounded):
    assert (COL_QM * LANES) % M_WIDTH == 0
    return pl.pallas_call(
        functools.partial(_attn_m_body, bounded),
        grid=(batch,),
        in_specs=[
            pl.BlockSpec((SEQ, M_WIDTH), lambda b: (b, COL_QM * LANES // M_WIDTH)),
            pl.BlockSpec((MEM_LEN, 2 * M_WIDTH), lambda b: (b, 0)),
        ],
        out_specs=pl.BlockSpec((SEQ, M_WIDTH), lambda b: (b, 0)),
        out_shape=jax.ShapeDtypeStruct((batch * SEQ, M_WIDTH), BF16),
        compiler_params=pltpu.CompilerParams(
            dimension_semantics=("parallel",), vmem_limit_bytes=VMEM_LIMIT),
        name="attn_memory_bounded" if bounded else "attn_memory",
    )(proj, kv)


MERGE_TM = 256


def _merge_body(x_ref, oa_ref, za_ref, ob_ref, zb_ref, om_ref, zm_ref, gla_ref, glb_ref, glm_ref, glt_ref, bg_ref,
                wa_ref, wb_ref, wm_ref, wo_ref, out_ref):
    def branch(o_ref, z_ref, w_ref):
        z = z_ref[...].astype(F32)
        gated = o_ref[...].astype(F32) * (z * jax.nn.sigmoid(z))
        return jnp.dot(gated.astype(BF16), w_ref[...], preferred_element_type=F32)

    gate_logits = (lambda: gla_ref[...], lambda: glb_ref[...],
                   lambda: jnp.concatenate([glm_ref[...], glt_ref[...]], axis=1))
    merged = None
    for n, (o_ref, z_ref, w_ref) in enumerate(((oa_ref, za_ref, wa_ref), (ob_ref, zb_ref, wb_ref),
                                                (om_ref, zm_ref, wm_ref))):
        cols = slice(n * D_MODEL, (n + 1) * D_MODEL)
        gate = jax.nn.sigmoid(gate_logits[n]().astype(F32) + bg_ref[:, cols])
        term = gate * branch(o_ref, z_ref, w_ref)
        merged = term if merged is None else merged + term
    out_ref[...] = x_ref[...] + jnp.dot(merged.astype(BF16), wo_ref[...], preferred_element_type=F32)


def _merge(x2, oa, ob, om, proj, gl_tail, bg, wa, wb, wm, wo):
    rows = x2.shape[0]
    row_blk = lambda width, colblk: pl.BlockSpec((MERGE_TM, width), lambda i: (i, colblk))
    resident = lambda shape: pl.BlockSpec(shape, lambda i: (0, 0), pipeline_mode=pl.Buffered(1))
    assert (COL_ZA * LANES) % A_WIDTH == 0 and (COL_ZB * LANES) % B_WIDTH == 0
    assert (COL_ZM * LANES) % M_WIDTH == 0 and (COL_GL * LANES) % D_MODEL == 0
    assert gl_tail.shape[1] == PROJ_TN and D_MODEL % PROJ_TN == 0
    gl_col = COL_GL * LANES // D_MODEL
    gl_m_width = D_MODEL - PROJ_TN
    return pl.pallas_call(
        _merge_body,
        grid=(rows // MERGE_TM,),
        in_specs=[
            row_blk(D_MODEL, 0),
            row_blk(A_WIDTH, 0), row_blk(A_WIDTH, COL_ZA * LANES // A_WIDTH),
            row_blk(B_WIDTH, 0), row_blk(B_WIDTH, COL_ZB * LANES // B_WIDTH),
            row_blk(M_WIDTH, 0), row_blk(M_WIDTH, COL_ZM * LANES // M_WIDTH),
            row_blk(D_MODEL, gl_col), row_blk(D_MODEL, gl_col + 1),
            row_blk(gl_m_width, (gl_col + 2) * D_MODEL // gl_m_width), row_blk(PROJ_TN, 0),
            resident((1, N_BRANCH * D_MODEL)),
            resident((A_WIDTH, D_MODEL)), resident((B_WIDTH, D_MODEL)), resident((M_WIDTH, D_MODEL)),
            resident((D_MODEL, D_MODEL)),
        ],
        out_specs=row_blk(D_MODEL, 0),
        out_shape=jax.ShapeDtypeStruct((rows, D_MODEL), F32),
        compiler_params=pltpu.CompilerParams(
            dimension_semantics=("parallel",), vmem_limit_bytes=VMEM_LIMIT),
        name="merge_out",
    )(x2, oa, proj, ob, proj, om, proj, proj, proj, proj, gl_tail, bg, wa, wb, wm, wo)


def _layer(x2, mem2, batch, layer, norm_g, mem_norm_g, w_in, b_forget, b_gate, rel_bias, q_norm_a, k_norm_a,
           q_norm_b, k_norm_b, q_norm_m, k_norm_m, w_mem_kv, w_proj_a, w_proj_b, w_proj_m, w_out):
    b_f = jnp.pad(b_forget.astype(F32), (0, LANES - B_HEADS))[None]
    scale = LOG2E / math.sqrt(HEAD_DIM)
    scale_m = LOG2E / math.sqrt(M_HEAD_DIM)
    ones = lambda n: jnp.ones((n,), F32)
    gain = jnp.concatenate([
        jnp.tile(q_norm_a.astype(F32) * scale, A_HEADS), jnp.tile(k_norm_a.astype(F32), A_HEADS), ones(2 * A_WIDTH),
        jnp.tile(q_norm_b.astype(F32) * scale, B_HEADS), jnp.tile(k_norm_b.astype(F32), B_HEADS), ones(2 * B_WIDTH),
        jnp.tile(q_norm_m.astype(F32) * scale_m, M_HEADS), ones(PROJ_COLS - COL_ZM * LANES)])[None]

    w_t = jnp.swapaxes(w_in.astype(F32), 1, 2)
    h, logf, gl_tail = _norm(x2, norm_g.astype(F32)[None], w_t, layer, b_f)
    proj = _proj(h, w_t, layer, gain)

    def logit_bound(gq, gk, dim, scl):
        return dim * scl * jnp.max(jnp.abs(gq.astype(F32))) * jnp.max(jnp.abs(gk.astype(F32)))

    def dispatch(bound, call, *operands):
        return lax.cond(bound <= LOGIT_RANGE, functools.partial(call, bounded=True),
                        functools.partial(call, bounded=False), *operands)

    bound_a = logit_bound(q_norm_a, k_norm_a, HEAD_DIM, scale) + LOG2E * jnp.max(jnp.abs(rel_bias.astype(F32)))
    oa = dispatch(bound_a, lambda p, b, bounded: _attn_a(p, b, batch, bounded),
                  proj, _bias_tiles(rel_bias.astype(F32)))

    c, ccol = _cumsum(logf, batch)
    ob = dispatch(logit_bound(q_norm_b, k_norm_b, HEAD_DIM, scale),
                  lambda p, c_, cc, bounded: _attn_b(p, c_, cc, batch, bounded), proj, c, ccol)

    kv = _mem_kv(mem2, mem_norm_g.astype(F32)[None], w_mem_kv.astype(BF16),
                 jnp.tile(k_norm_m.astype(F32), M_HEADS)[None], batch)
    om = dispatch(logit_bound(q_norm_m, k_norm_m, M_HEAD_DIM, scale_m),
                  lambda p, kv_, bounded: _attn_m(p, kv_, batch, bounded), proj, kv)

    return _merge(x2, oa, ob, om, proj, gl_tail, b_gate.astype(F32).reshape(1, N_BRANCH * D_MODEL),
                  w_proj_a.astype(BF16), w_proj_b.astype(BF16), w_proj_m.astype(BF16), w_out.astype(BF16))


def kernel(x, mem, norm_g, mem_norm_g, w_in, b_forget, b_gate, rel_bias, q_norm_a, k_norm_a, q_norm_b, k_norm_b,
           q_norm_m, k_norm_m, w_mem_kv, w_proj_a, w_proj_b, w_proj_m, w_out):
    batch, seq, d_model = x.shape
    assert (seq, d_model) == (SEQ, D_MODEL) and mem.shape == (batch, MEM_LEN, D_MODEL)
    x2 = x.reshape(batch * seq, d_model)
    mem2 = mem.reshape(batch * MEM_LEN, d_model)
    depth = w_in.shape[0]
    for l in range(depth):
        x2 = _layer(x2, mem2, batch, l, norm_g[l], mem_norm_g[l], w_in, b_forget[l], b_gate[l], rel_bias,
                    q_norm_a[l], k_norm_a[l], q_norm_b[l], k_norm_b[l], q_norm_m[l], k_norm_m[l],
                    w_mem_kv[l], w_proj_a[l], w_proj_b[l], w_proj_m[l], w_out[l])
    return x2.reshape(batch, seq, d_model)
```
